```python
import jax
import jax.numpy as jnp
from jax import lax
import numpy as np

D_MODEL = 1024
BATCH = 2
SEQ = 16384
DEPTH = 2

RET_HEADS = 4
RET_HD = 64
GDN_HEADS = 4
GDN_HD = 128
HG_HEADS = 4
HG_HD = 64
RET_W = RET_HEADS * RET_HD
GDN_W = GDN_HEADS * GDN_HD
HG_W = HG_HEADS * HG_HD
MIX_W = RET_W + GDN_W + HG_W
CHUNK = 64
CONV_W = 4
ROPE_BASE = 10000.0
N_GROUPS = 4
EXP_PER_GROUP = 8
N_EXPERTS = N_GROUPS * EXP_PER_GROUP
TOP_K = 2
D_EXPERT = 512
MOE_BLOCK = 256
EPS = 1e-6

_SPLITS = (RET_W,) * 4 + (GDN_W,) * 3 + (GDN_HEADS, GDN_HEADS, GDN_W) + (HG_W,) * 4
IN_W = sum(_SPLITS)
_SPLIT_IDX = tuple(int(v) for v in np.cumsum(_SPLITS)[:-1])

kernel_name = 'hybrid_ret_gdn_hgrn2_hmoe'


def rms_norm(x, g):
    xf = x.astype(jnp.float32)
    y = xf * lax.rsqrt(jnp.mean(xf * xf, axis=-1, keepdims=True) + EPS)
    return (y * g.astype(jnp.float32)).astype(x.dtype)


def head_norm(o, gain):
    o = o * lax.rsqrt(jnp.mean(o * o, axis=-1, keepdims=True) + EPS)
    return o * gain.astype(jnp.float32)


def l2norm(t):
    return t * lax.rsqrt(jnp.sum(t * t, axis=-1, keepdims=True) + EPS)


def rotary(t, pos):
    d = t.shape[-1]
    inv = ROPE_BASE ** (-jnp.arange(0, d, 2, dtype=jnp.float32) / d)
    ang = pos.astype(jnp.float32)[:, None] * inv[None, :]
    cos = jnp.cos(ang)[None, :, None, :]
    sin = jnp.sin(ang)[None, :, None, :]
    t1, t2 = t[..., : d // 2], t[..., d // 2:]
    return jnp.concatenate([t1 * cos - t2 * sin, t1 * sin + t2 * cos], axis=-1)


def causal_conv(x, w):
    c = x.shape[-1]
    y = lax.conv_general_dilated(x, w[:, None, :].astype(x.dtype), window_strides=(1,),
                                 padding=[(CONV_W - 1, 0)],
                                 dimension_numbers=('NWC', 'WIO', 'NWC'),
                                 feature_group_count=c)
    return jax.nn.silu(y)


def to_chunks(t):
    b, s, h, d = t.shape
    return t.reshape(b, s // CHUNK, CHUNK, h, d).transpose(0, 3, 1, 2, 4)


def from_chunks(t):
    b, h, n, c, d = t.shape
    return t.transpose(0, 2, 3, 1, 4).reshape(b, n * c, h, d)


def retention(q, k, v):
    h = q.shape[2]
    log_gamma = jnp.log(1.0 - 2.0 ** (-5.0 - jnp.arange(h, dtype=jnp.float32)))
    q = to_chunks(q)
    k = to_chunks(k) * (RET_HD ** -0.5)
    v = to_chunks(v)
    idx = jnp.arange(CHUNK, dtype=jnp.float32)
    diff = idx[:, None] - idx[None, :]
    causal = diff >= 0
    dmat = jnp.where(causal, jnp.exp(log_gamma[:, None, None] * jnp.where(causal, diff, 0.0)), 0.0)
    scores = jnp.einsum('bhncd,bhnsd->bhncs', q, k) * dmat[None, :, None]
    o = jnp.einsum('bhncs,bhnse->bhnce', scores, v)
    zeta = jnp.exp(log_gamma[:, None] * (CHUNK - 1.0 - idx))
    kv = jnp.einsum('bhnsd,hs,bhnse->bhnde', k, zeta, v)
    chunk_decay = jnp.exp(log_gamma * CHUNK)[None, :, None, None]

    def step(state, kv_n):
        return state * chunk_decay + kv_n, state

    _, prev = lax.scan(step, jnp.zeros_like(kv[:, :, 0]), jnp.moveaxis(kv, 2, 0))
    prev = jnp.moveaxis(prev, 0, 2)
    xi = jnp.exp(log_gamma[:, None] * (idx + 1.0))
    o = o + jnp.einsum('bhncd,hc,bhnde->bhnce', q, xi, prev)
    return from_chunks(o)


def gated_delta_rule(q, k, v, g, beta):
    dk = q.shape[-1]
    q = to_chunks(q) * (dk ** -0.5)
    k = to_chunks(k)
    v = to_chunks(v)
    dv = v.shape[-1]
    g = to_chunks(g[..., None])[..., 0]
    beta = to_chunks(beta[..., None])[..., 0]
    big_g = jnp.cumsum(g, axis=-1)
    causal = jnp.tril(jnp.ones((CHUNK, CHUNK), dtype=bool))
    strict = jnp.tril(jnp.ones((CHUNK, CHUNK), dtype=bool), -1)
    gamma = jnp.exp(jnp.where(causal, big_g[..., :, None] - big_g[..., None, :], -jnp.inf))
    kk = jnp.einsum('bhncd,bhnsd->bhncs', k, k)
    a_mat = jnp.where(strict, beta[..., :, None] * kk * gamma, 0.0) + jnp.eye(CHUNK, dtype=jnp.float32)
    rhs = jnp.concatenate([v * beta[..., None], k * (beta * jnp.exp(big_g))[..., None]], axis=-1)
    sol = lax.linalg.triangular_solve(a_mat, rhs, left_side=True, lower=True, unit_diagonal=True)
    u, w = sol[..., :dv], sol[..., dv:]
    qk = jnp.einsum('bhncd,bhnsd->bhncs', q, k) * gamma
    q_dec = q * jnp.exp(big_g)[..., None]
    k_dec = k * jnp.exp(big_g[..., -1:] - big_g)[..., None]
    g_last = jnp.exp(big_g[..., -1])

    def step(state, inp):
        u_n, w_n, qk_n, qd_n, kd_n, gl_n = inp
        v_new = u_n - jnp.einsum('bhcd,bhde->bhce', w_n, state)
        o = jnp.einsum('bhcd,bhde->bhce', qd_n, state) + jnp.einsum('bhcs,bhse->bhce', qk_n, v_new)
        state = state * gl_n[..., None, None] + jnp.einsum('bhcd,bhce->bhde', kd_n, v_new)
        return state, o

    xs = tuple(jnp.moveaxis(t, 2, 0) for t in (u, w, qk, q_dec, k_dec, g_last))
    s0 = jnp.zeros((q.shape[0], q.shape[1], dk, dv), jnp.float32)
    _, o = lax.scan(step, s0, xs)
    return from_chunks(jnp.moveaxis(o, 0, 2))


def hgrn2(q, k, v, g):
    q, k, v, g = to_chunks(q), to_chunks(k), to_chunks(v), to_chunks(g)
    big_g = jnp.cumsum(g, axis=3)
    causal = jnp.tril(jnp.ones((CHUNK, CHUNK), dtype=bool))[:, :, None]

    def step(state, inp):
        q_n, k_n, v_n, g_n = inp
        gd = g_n[:, :, :, None, :] - g_n[:, :, None, :, :]
        decay = jnp.exp(jnp.where(causal, gd, -jnp.inf))
        att = jnp.einsum('bhcd,bhcsd,bhsd->bhcs', q_n, decay, k_n)
        o = jnp.einsum('bhcs,bhse->bhce', att, v_n) + jnp.einsum('bhcd,bhde->bhce', q_n * jnp.exp(g_n), state)
        g_last = g_n[:, :, -1]
        state = state * jnp.exp(g_last)[..., None] + jnp.einsum(
            'bhsd,bhse->bhde', k_n * jnp.exp(g_last[:, :, None] - g_n), v_n)
        return state, o

    xs = tuple(jnp.moveaxis(t, 2, 0) for t in (q, k, v, big_g))
    s0 = jnp.zeros((q.shape[0], q.shape[1], q.shape[-1], v.shape[-1]), jnp.float32)
    _, o = lax.scan(step, s0, xs)
    return from_chunks(jnp.moveaxis(o, 0, 2))


def hybrid_mixer(h, w_in, w_out, conv_w, a_log, dt_bias, ret_g, gdn_g, hg_g, lb, pos):
    b, s, _ = h.shape
    f32 = jnp.float32
    proj = h @ w_in
    (rq, rk, rv, rg, dq, dk, dv, da, db, dz, hq, hf, hi, hgt) = jnp.split(proj, _SPLIT_IDX, axis=-1)

    def heads(t, n):
        return t.astype(f32).reshape(b, s, n, -1)

    r = retention(rotary(heads(rq, RET_HEADS), pos), rotary(heads(rk, RET_HEADS), pos), heads(rv, RET_HEADS))
    r = head_norm(r, ret_g.reshape(RET_HEADS, RET_HD)) * jax.nn.silu(heads(rg, RET_HEADS))

    qkv = causal_conv(jnp.concatenate([dq, dk, dv], axis=-1), conv_w)
    cq, ck, cv = jnp.split(qkv, (GDN_W, 2 * GDN_W), axis=-1)
    decay = -jnp.exp(a_log.astype(f32)) * jax.nn.softplus(da.astype(f32) + dt_bias.astype(f32))
    beta = jax.nn.sigmoid(db.astype(f32))
    d = gated_delta_rule(l2norm(heads(cq, GDN_HEADS)), l2norm(heads(ck, GDN_HEADS)),
                         heads(cv, GDN_HEADS), decay, beta)
    d = head_norm(d, gdn_g) * jax.nn.silu(heads(dz, GDN_HEADS))

    lbf = lb.astype(f32)
    fgate = lbf + (1.0 - lbf) * jax.nn.sigmoid(hf.astype(f32))
    c = hgrn2(jax.nn.silu(heads(hq, HG_HEADS)),
              (1.0 - fgate).reshape(b, s, HG_HEADS, HG_HD),
              heads(hi, HG_HEADS),
              jnp.log(fgate).reshape(b, s, HG_HEADS, HG_HD))
    c = head_norm(c, hg_g.reshape(HG_HEADS, HG_HD)) * jax.nn.silu(heads(hgt, HG_HEADS))

    mixed = jnp.concatenate([r.reshape(b, s, RET_W), d.reshape(b, s, GDN_W), c.reshape(b, s, HG_W)], axis=-1)
    return mixed.astype(h.dtype) @ w_out


def hier_moe(h, w_rg, w_re, w_gate, w_up, w_down):
    b, s, dm = h.shape
    t = b * s
    xt = h.reshape(t, dm)
    pg = jax.nn.softmax((xt @ w_rg).astype(jnp.float32), axis=-1)
    pg_top, grp = lax.top_k(pg, 1)
    el = (xt @ w_re).astype(jnp.float32).reshape(t, N_GROUPS, EXP_PER_GROUP)
    el_sel = jnp.take_along_axis(el, grp[:, :, None], axis=1)[:, 0]
    pe = jax.nn.softmax(el_sel, axis=-1)
    top_p, top_i = lax.top_k(pe, TOP_K)
    top_p = top_p / jnp.sum(top_p, axis=-1, keepdims=True)
    weights = pg_top * top_p
    expert = grp * EXP_PER_GROUP + top_i

    n_assign = t * TOP_K
    e_flat = expert.reshape(n_assign)
    t_flat = jnp.repeat(jnp.arange(t, dtype=jnp.int32), TOP_K)
    w_flat = weights.reshape(n_assign)
    order = jnp.argsort(e_flat)
    e_sorted = e_flat[order]
    counts = jnp.zeros((N_EXPERTS,), jnp.int32).at[e_flat].add(1)
    padded = (counts + MOE_BLOCK - 1) // MOE_BLOCK * MOE_BLOCK
    start_raw = jnp.cumsum(counts) - counts
    ends_pad = jnp.cumsum(padded)
    start_pad = ends_pad - padded
    dest = start_pad[e_sorted] + jnp.arange(n_assign, dtype=jnp.int32) - start_raw[e_sorted]
    n_blocks = (n_assign + N_EXPERTS * (MOE_BLOCK - 1) + MOE_BLOCK - 1) // MOE_BLOCK
    n_slots = n_blocks * MOE_BLOCK
    slot_tok = jnp.full((n_slots,), t, jnp.int32).at[dest].set(t_flat[order])
    slot_w = jnp.zeros((n_slots,), jnp.float32).at[dest].set(w_flat[order])
    block_exp = jnp.minimum(
        jnp.searchsorted(ends_pad, jnp.arange(n_blocks, dtype=jnp.int32) * MOE_BLOCK, side='right'),
        N_EXPERTS - 1)
    x_pad = jnp.concatenate([xt, jnp.zeros((1, dm), xt.dtype)], axis=0)

    def run_block(args):
        tok, e = args
        xb = x_pad[tok]
        hb = jax.nn.silu(xb @ w_gate[e]) * (xb @ w_up[e])
        return hb @ w_down[e]

    yb = lax.map(run_block, (slot_tok.reshape(n_blocks, MOE_BLOCK), block_exp))
    yb = yb.reshape(n_slots, dm)
    y = jax.ops.segment_sum(yb * slot_w[:, None].astype(yb.dtype), slot_tok, num_segments=t + 1)[:t]
    return y.reshape(b, s, dm)


def setup_inputs(seed: int = 0) -> dict:
    key = jax.random.key(seed)
    ks = jax.random.split(key, 20)
    f32 = jnp.float32

    def nrm(k, shape, scale):
        return jax.random.normal(k, shape, f32) * scale

    return {
        'x': nrm(ks[0], (BATCH, SEQ, D_MODEL), 1.0),
        'w_in': nrm(ks[1], (DEPTH, D_MODEL, IN_W), D_MODEL ** -0.5),
        'w_out': nrm(ks[2], (DEPTH, MIX_W, D_MODEL), MIX_W ** -0.5),
        'gdn_conv': nrm(ks[3], (DEPTH, CONV_W, 3 * GDN_W), CONV_W ** -0.5),
        'gdn_a_log': jnp.log(jax.random.uniform(ks[4], (DEPTH, GDN_HEADS), f32, 1.0, 16.0)),
        'gdn_dt_bias': nrm(ks[5], (DEPTH, GDN_HEADS), 0.5) - 2.0,
        'ret_norm': 1.0 + nrm(ks[6], (DEPTH, RET_W), 0.02),
        'gdn_norm': 1.0 + nrm(ks[7], (DEPTH, GDN_HD), 0.02),
        'hgrn_norm': 1.0 + nrm(ks[8], (DEPTH, HG_W), 0.02),
        'hgrn_lower_bounds': nrm(ks[9], (DEPTH, HG_W), 0.5),
        'attn_norm': 1.0 + nrm(ks[10], (DEPTH, D_MODEL), 0.02),
        'ffn_norm': 1.0 + nrm(ks[11], (DEPTH, D_MODEL), 0.02),
        'router_group': nrm(ks[12], (DEPTH, D_MODEL, N_GROUPS), D_MODEL ** -0.5),
        'router_expert': nrm(ks[13], (DEPTH, D_MODEL, N_EXPERTS), D_MODEL ** -0.5),
        'w_gate': nrm(ks[14], (DEPTH, N_EXPERTS, D_MODEL, D_EXPERT), D_MODEL ** -0.5),
        'w_up': nrm(ks[15], (DEPTH, N_EXPERTS, D_MODEL, D_EXPERT), D_MODEL ** -0.5),
        'w_down': nrm(ks[16], (DEPTH, N_EXPERTS, D_EXPERT, D_MODEL), D_EXPERT ** -0.5),
        'final_norm': 1.0 + nrm(ks[17], (D_MODEL,), 0.02),
    }


def reference(x, w_in, w_out, gdn_conv, gdn_a_log, gdn_dt_bias, ret_norm, gdn_norm, hgrn_norm,
              hgrn_lower_bounds, attn_norm, ffn_norm, router_group, router_expert,
              w_gate, w_up, w_down, final_norm):
    pos = jnp.arange(x.shape[1], dtype=jnp.int32)
    lb_all = jnp.cumsum(jax.nn.softmax(hgrn_lower_bounds.astype(jnp.float32), axis=0), axis=0)
    lb_all = lb_all - lb_all[0:1]
    h = x
    for l in range(DEPTH):
        h = h + hybrid_mixer(rms_norm(h, attn_norm[l]), w_in[l], w_out[l], gdn_conv[l], gdn_a_log[l],
                             gdn_dt_bias[l], ret_norm[l], gdn_norm[l], hgrn_norm[l], lb_all[l], pos)
        h = h + hier_moe(rms_norm(h, ffn_norm[l]), router_group[l], router_expert[l],
                         w_gate[l], w_up[l], w_down[l])
    return rms_norm(h, final_norm)
```

```python
import functools

import jax
import jax.numpy as jnp
import numpy as np
from jax import lax
from jax.experimental import pallas as pl
from jax.experimental.pallas import tpu as pltpu

D_MODEL = 1024
DEPTH = 2
RET_HEADS, RET_HD = 4, 64
GDN_HEADS, GDN_HD = 4, 128
HG_HEADS, HG_HD = 4, 64
RET_W = RET_HEADS * RET_HD
GDN_W = GDN_HEADS * GDN_HD
HG_W = HG_HEADS * HG_HD
MIX_W = RET_W + GDN_W + HG_W
CHUNK = 64
CONV_W = 4
ROPE_BASE = 10000.0
N_GROUPS = 4
EXP_PER_GROUP = 8
N_EXPERTS = N_GROUPS * EXP_PER_GROUP
TOP_K = 2
D_EXPERT = 512
MOE_BLOCK = 256
EPS = 1e-6

LANES = 128
SUB_BLOCK = 16

OFF_RET = 0
OFF_GDN = 4 * RET_W
OFF_HG = OFF_GDN + 4 * GDN_W
OFF_SMALL = OFF_HG + 4 * HG_W
PROJ_W = OFF_SMALL + LANES

VMEM_LIMIT = 56 * 1024 * 1024

F32 = jnp.float32
BF16 = jnp.bfloat16


def _mm(a, b):
    return jnp.dot(a.astype(BF16), b.astype(BF16), preferred_element_type=F32)


def _mm_nt(a, b):
    return lax.dot_general(a.astype(BF16), b.astype(BF16), (((1,), (1,)), ((), ())),
                           preferred_element_type=F32)


def _mm_tn(a, b):
    return lax.dot_general(a.astype(BF16), b.astype(BF16), (((0,), (0,)), ((), ())),
                           preferred_element_type=F32)


def _mm_f32(a, b):
    return jnp.dot(a, b, preferred_element_type=F32, precision=lax.Precision.HIGHEST)


def _sigmoid(x):
    return jax.nn.sigmoid(x)


def _silu(x):
    return x * _sigmoid(x)


def _softplus(x):
    return jnp.maximum(x, 0.0) + jnp.log(1.0 + jnp.exp(-jnp.abs(x)))


def _inproj_kernel(*refs, n_add):
    if n_add:
        h_ref, ya_ref, yb_ref, g_ref, w_ref, proj_ref, hout_ref = refs
        x = h_ref[...] + ya_ref[...] + yb_ref[...]
        hout_ref[...] = x
    else:
        h_ref, g_ref, w_ref, proj_ref = refs
        x = h_ref[...]
    ms = jnp.mean(x * x, axis=-1, keepdims=True)
    xn = x * lax.rsqrt(ms + EPS) * g_ref[...]
    proj_ref[...] = jnp.dot(xn.astype(BF16), w_ref[...], preferred_element_type=F32)


def _inproj(h, y2, gain, w_bf16, tm):
    t = h.shape[0]
    n_add = y2 is not None
    row = pl.BlockSpec((tm, D_MODEL), lambda i: (i, 0))
    in_specs = [row]
    args = [h]
    if n_add:
        in_specs += [pl.BlockSpec((None, tm, D_MODEL), lambda i: (0, i, 0)),
                     pl.BlockSpec((None, tm, D_MODEL), lambda i: (1, i, 0))]
        args += [y2, y2]
    in_specs += [pl.BlockSpec((1, D_MODEL), lambda i: (0, 0)),
                 pl.BlockSpec((D_MODEL, PROJ_W), lambda i: (0, 0), pipeline_mode=pl.Buffered(1))]
    args += [gain.reshape(1, D_MODEL), w_bf16]
    out_shape = [jax.ShapeDtypeStruct((t, PROJ_W), F32)]
    out_specs = [pl.BlockSpec((tm, PROJ_W), lambda i: (i, 0))]
    if n_add:
        out_shape.append(jax.ShapeDtypeStruct((t, D_MODEL), F32))
        out_specs.append(row)
    res = pl.pallas_call(
        functools.partial(_inproj_kernel, n_add=n_add),
        grid=(t // tm,),
        in_specs=in_specs,
        out_specs=out_specs,
        out_shape=out_shape,
        compiler_params=pltpu.CompilerParams(dimension_semantics=("arbitrary",),
                                             vmem_limit_bytes=VMEM_LIMIT),
        name="inproj",
    )(*args)
    return res if n_add else res[0]


def _swap_halves(x):
    lane = lax.broadcasted_iota(jnp.int32, (x.shape[0], LANES), 1)
    first = (lane % RET_HD) < (RET_HD // 2)
    parts = []
    for p in range(x.shape[1] // LANES):
        xp = x[:, LANES * p:LANES * (p + 1)]
        parts.append(jnp.where(first, pltpu.roll(xp, LANES - RET_HD // 2, 1), pltpu.roll(xp, RET_HD // 2, 1)))
    return jnp.concatenate(parts, axis=1)


def _mixer_kernel(proj_ref, cos_ref, sin_ref, dmat_ref, xi_ref, zeta_ref, rdec_ref, bd_ref, tri_ref,
                  convw_ref, alog_ref, dtb_ref, retg_ref, gdng_ref, hgg_ref, lb_ref,
                  out_ref, ret_s, gdn_s, hg_s, conv_buf):
    c = CHUNK

    @pl.when(pl.program_id(1) == 0)
    def _():
        ret_s[...] = jnp.zeros_like(ret_s)
        gdn_s[...] = jnp.zeros_like(gdn_s)
        hg_s[...] = jnp.zeros_like(hg_s)
        conv_buf[0:8, :] = jnp.zeros((8, 3 * GDN_W), F32)

    bd = bd_ref[...]
    tri = tri_ref[...]
    r64 = lax.broadcasted_iota(jnp.int32, (c, c), 0)
    c64 = lax.broadcasted_iota(jnp.int32, (c, c), 1)
    causal = r64 >= c64
    strict = r64 > c64
    eye = (r64 == c64).astype(F32)
    head4 = lax.broadcasted_iota(jnp.int32, (c, RET_W), 1) // RET_HD

    q = proj_ref[:, OFF_RET:OFF_RET + RET_W]
    k = proj_ref[:, OFF_RET + RET_W:OFF_RET + 2 * RET_W]
    v = proj_ref[:, OFF_RET + 2 * RET_W:OFF_RET + 3 * RET_W]
    rg = proj_ref[:, OFF_RET + 3 * RET_W:OFF_RET + 4 * RET_W]
    cos_t = cos_ref[...]
    sin_t = sin_ref[...]
    qr = q * cos_t + _swap_halves(q) * sin_t
    kr = (k * cos_t + _swap_halves(k) * sin_t) * (RET_HD ** -0.5)
    s_ret = ret_s[...]
    o = _mm(qr * xi_ref[...], s_ret)
    for h in range(RET_HEADS):
        mh = head4 == h
        sc = _mm_nt(jnp.where(mh, qr, 0.0), kr) * dmat_ref[h]
        o = o + jnp.where(mh, _mm(sc, v), 0.0)
    ret_s[...] = s_ret * rdec_ref[...] + bd * _mm_tn(kr * zeta_ref[...], v)
    ms = _mm_f32(o * o, bd) * (1.0 / RET_HD)
    out_ref[:, 0:RET_W] = o * lax.rsqrt(ms + EPS) * retg_ref[...] * _silu(rg)

    conv_buf[8:8 + c, :] = proj_ref[:, OFF_GDN:OFF_GDN + 3 * GDN_W]
    y = conv_buf[8:8 + c, :] * convw_ref[CONV_W - 1:CONV_W, :]
    for j in range(CONV_W - 1):
        lo = 8 - (CONV_W - 1) + j
        y = y + conv_buf[lo:lo + c, :] * convw_ref[j:j + 1, :]
    conv_buf[0:8, :] = conv_buf[c:c + 8, :]
    y = _silu(y)

    small = proj_ref[:, OFF_SMALL:OFF_SMALL + LANES]
    gv = -jnp.exp(alog_ref[...]) * _softplus(small + dtb_ref[...])
    bv = _sigmoid(small)
    gcum = _mm_f32(tri, gv)
    sel = (lax.broadcasted_iota(jnp.int32, (8, LANES), 0) == lax.broadcasted_iota(jnp.int32, (8, LANES), 1)).astype(F32)
    grow_all = lax.dot_general(sel, gcum, (((1,), (1,)), ((), ())), preferred_element_type=F32,
                               precision=lax.Precision.HIGHEST)

    for h in range(GDN_HEADS):
        cq = y[:, GDN_HD * h:GDN_HD * (h + 1)]
        ck = y[:, GDN_W + GDN_HD * h:GDN_W + GDN_HD * (h + 1)]
        cv = y[:, 2 * GDN_W + GDN_HD * h:2 * GDN_W + GDN_HD * (h + 1)]
        qn = cq * lax.rsqrt(jnp.sum(cq * cq, axis=-1, keepdims=True) + EPS) * (GDN_HD ** -0.5)
        kn = ck * lax.rsqrt(jnp.sum(ck * ck, axis=-1, keepdims=True) + EPS)
        gcol = gcum[:, h:h + 1]
        grow = grow_all[h:h + 1, :]
        bcol = bv[:, GDN_HEADS + h:GDN_HEADS + h + 1]
        gamma = jnp.where(causal, jnp.exp(jnp.minimum(gcol - grow, 0.0)), 0.0)
        a_m = jnp.where(strict, bcol * _mm_nt(kn, kn) * gamma, 0.0)
        t_inv = eye - a_m
        p = a_m
        for _ in range(5):
            p = _mm(p, p)
            t_inv = t_inv + _mm(t_inv, p)
        eg = jnp.exp(gcol)
        sol = _mm(t_inv, jnp.concatenate([cv * bcol, kn * (bcol * eg)], axis=1))
        u = sol[:, :GDN_HD]
        w = sol[:, GDN_HD:]
        qk = _mm_nt(qn, kn) * gamma
        glast = gcum[c - 1:c, h:h + 1]
        kd = kn * jnp.exp(glast - gcol)
        s_g = gdn_s[h]
        v_new = u - _mm(w, s_g)
        oh = _mm(qn * eg, s_g) + _mm(qk, v_new)
        gdn_s[h] = s_g * jnp.exp(glast) + _mm_tn(kd, v_new)
        msd = jnp.mean(oh * oh, axis=-1, keepdims=True)
        dz = proj_ref[:, OFF_GDN + 3 * GDN_W + GDN_HD * h:OFF_GDN + 3 * GDN_W + GDN_HD * (h + 1)]
        out_ref[:, RET_W + GDN_HD * h:RET_W + GDN_HD * (h + 1)] = (
            oh * lax.rsqrt(msd + EPS) * gdng_ref[...] * _silu(dz))

    hq = proj_ref[:, OFF_HG:OFF_HG + HG_W]
    hf = proj_ref[:, OFF_HG + HG_W:OFF_HG + 2 * HG_W]
    hv = proj_ref[:, OFF_HG + 2 * HG_W:OFF_HG + 3 * HG_W]
    hgt = proj_ref[:, OFF_HG + 3 * HG_W:OFF_HG + 4 * HG_W]
    lb = lb_ref[...]
    qh = _silu(hq)
    f = lb + (1.0 - lb) * _sigmoid(hf)
    kh = 1.0 - f
    gl = _mm_f32(tri, jnp.log(f))
    row = lax.broadcasted_iota(jnp.int32, (c, HG_W), 0)
    blk = row // SUB_BLOCK
    rin = row % SUB_BLOCK
    nblk = c // SUB_BLOCK
    s_h = hg_s[...]
    o = _mm_nt(qh * jnp.exp(gl), s_h)
    gref = gl[SUB_BLOCK - 1:SUB_BLOCK, :]
    for i in range(2, nblk):
        gref = jnp.where(blk >= i, gl[SUB_BLOCK * i - 1:SUB_BLOCK * i, :], gref)
    qs = qh * jnp.exp(jnp.minimum(gl - gref, 0.0))
    ks = []
    for i in range(1, nblk):
        gi = gl[SUB_BLOCK * i - 1:SUB_BLOCK * i, :]
        ks.append(jnp.where(row < SUB_BLOCK * i, kh * jnp.exp(jnp.minimum(gi - gl, 0.0)), 0.0))
    kcat = jnp.concatenate(ks, axis=1)
    for h in range(HG_HEADS):
        mh = head4 == h
        qcat = jnp.concatenate([jnp.where(mh & (blk == i), qs, 0.0) for i in range(1, nblk)], axis=1)
        o = o + jnp.where(mh, _mm(_mm_nt(qcat, kcat), hv), 0.0)
    ps = [jnp.where(rin >= 0, qh * kh, 0.0)]
    for j in range(1, SUB_BLOCK):
        e = jnp.exp(jnp.minimum(gl - pltpu.roll(gl, j, 0), 0.0))
        ps.append(jnp.where(rin >= j, qh * e * pltpu.roll(kh, j, 0), 0.0))
    acat = _mm(jnp.concatenate(ps, axis=0), bd)
    o = o + acat[0:c] * hv
    for j in range(1, SUB_BLOCK):
        o = o + acat[c * j:c * (j + 1)] * pltpu.roll(hv, j, 0)
    glast = gl[c - 1:c, :]
    hg_s[...] = s_h * jnp.exp(glast) + bd * _mm_tn(hv, kh * jnp.exp(glast - gl))
    ms = _mm_f32(o * o, bd) * (1.0 / HG_HD)
    out_ref[:, RET_W + GDN_W:MIX_W] = o * lax.rsqrt(ms + EPS) * hgg_ref[...] * _silu(hgt)


def _mixer_consts(seq):
    hd = RET_HD
    inv = ROPE_BASE ** (-np.arange(0, hd, 2, dtype=np.float32) / hd)
    ang = np.arange(seq, dtype=np.float32)[:, None] * inv[None, :]
    ang = jnp.arange(seq, dtype=F32)[:, None] * jnp.asarray(inv)[None, :]
    cos = jnp.cos(ang)
    sin = jnp.sin(ang)
    cos_t = jnp.tile(jnp.concatenate([cos, cos], axis=1), (1, RET_HEADS))
    sin_t = jnp.tile(jnp.concatenate([-sin, sin], axis=1), (1, RET_HEADS))
    log_gamma = jnp.log(1.0 - 2.0 ** (-5.0 - jnp.arange(RET_HEADS, dtype=F32)))
    idx = jnp.arange(CHUNK, dtype=F32)
    diff = idx[:, None] - idx[None, :]
    cm = diff >= 0
    dmat = jnp.where(cm, jnp.exp(log_gamma[:, None, None] * jnp.where(cm, diff, 0.0)), 0.0)
    zeta = jnp.exp(log_gamma[:, None] * (CHUNK - 1.0 - idx))
    xi = jnp.exp(log_gamma[:, None] * (idx + 1.0))
    zeta_t = jnp.repeat(zeta.T, RET_HD, axis=1)
    xi_t = jnp.repeat(xi.T, RET_HD, axis=1)
    rdec = jnp.repeat(jnp.exp(log_gamma * CHUNK), RET_HD)[None, :]
    hid = np.arange(RET_W) // RET_HD
    bd = jnp.asarray((hid[:, None] == hid[None, :]).astype(np.float32))
    tri = jnp.asarray(np.tril(np.ones((CHUNK, CHUNK), np.float32)))
    return cos_t, sin_t, dmat, xi_t, zeta_t, rdec, bd, tri


def _mixer(proj, consts, conv_w, a_log, dt_bias, ret_g, gdn_g, hg_g, lb, batch, seq):
    cos_t, sin_t, dmat, xi_t, zeta_t, rdec, bd, tri = consts
    n = seq // CHUNK
    pad4 = lambda a: jnp.zeros((1, LANES), F32).at[0, :GDN_HEADS].set(a.astype(F32))
    const2 = lambda shape: pl.BlockSpec(shape, lambda b, i: (0,) * len(shape))
    in_specs = [
        pl.BlockSpec((CHUNK, PROJ_W), lambda b, i: (b * n + i, 0)),
        pl.BlockSpec((CHUNK, RET_W), lambda b, i: (i, 0)),
        pl.BlockSpec((CHUNK, RET_W), lambda b, i: (i, 0)),
        const2((RET_HEADS, CHUNK, CHUNK)),
        const2((CHUNK, RET_W)),
        const2((CHUNK, RET_W)),
        const2((1, RET_W)),
        const2((RET_W, RET_W)),
        const2((CHUNK, CHUNK)),
        const2((CONV_W, 3 * GDN_W)),
        const2((1, LANES)),
        const2((1, LANES)),
        const2((1, RET_W)),
        const2((1, GDN_HD)),
        const2((1, HG_W)),
        const2((1, HG_W)),
    ]
    return pl.pallas_call(
        _mixer_kernel,
        grid=(batch, n),
        in_specs=in_specs,
        out_specs=pl.BlockSpec((CHUNK, MIX_W), lambda b, i: (b * n + i, 0)),
        out_shape=jax.ShapeDtypeStruct((batch * seq, MIX_W), F32),
        scratch_shapes=[
            pltpu.VMEM((RET_W, RET_W), F32),
            pltpu.VMEM((GDN_HEADS, GDN_HD, GDN_HD), F32),
            pltpu.VMEM((HG_W, HG_W), F32),
            pltpu.VMEM((CHUNK + 8, 3 * GDN_W), F32),
        ],
        compiler_params=pltpu.CompilerParams(dimension_semantics=("arbitrary", "arbitrary"),
                                             vmem_limit_bytes=VMEM_LIMIT),
        name="mixer",
    )(proj, cos_t, sin_t, dmat, xi_t, zeta_t, rdec, bd, tri,
      conv_w.astype(F32), pad4(a_log), pad4(dt_bias), ret_g.reshape(1, RET_W).astype(F32),
      gdn_g.reshape(1, GDN_HD).astype(F32), hg_g.reshape(1, HG_W).astype(F32), lb.reshape(1, HG_W).astype(F32))


def _post_kernel(mixed_ref, h_ref, wout_ref, g_ref, wr_ref, h1_ref, xn_ref, rout_ref):
    h1 = h_ref[...] + jnp.dot(mixed_ref[...].astype(BF16), wout_ref[...], preferred_element_type=F32)
    h1_ref[...] = h1
    ms = jnp.mean(h1 * h1, axis=-1, keepdims=True)
    xn = h1 * lax.rsqrt(ms + EPS) * g_ref[...]
    xn_ref[...] = xn
    logits = jnp.dot(xn.astype(BF16), wr_ref[...], preferred_element_type=F32)
    lane = lax.broadcasted_iota(jnp.int32, logits.shape, 1)
    big = jnp.int32(1 << 20)
    ninf = jnp.float32(-jnp.inf)
    gmask = lane < N_GROUPS
    gl = jnp.where(gmask, logits, ninf)
    eg = jnp.exp(gl - jnp.max(gl, axis=-1, keepdims=True))
    pg = eg / jnp.sum(eg, axis=-1, keepdims=True)
    pg_top = jnp.max(pg, axis=-1, keepdims=True)
    grp = jnp.min(jnp.where(gmask & (pg == pg_top), lane, big), axis=-1, keepdims=True)
    lo = N_GROUPS + EXP_PER_GROUP * grp
    emask = (lane >= lo) & (lane < lo + EXP_PER_GROUP)
    el = jnp.where(emask, logits, ninf)
    ee = jnp.exp(el - jnp.max(el, axis=-1, keepdims=True))
    pe = ee / jnp.sum(ee, axis=-1, keepdims=True)
    pe = jnp.where(emask, pe, -1.0)
    p1 = jnp.max(pe, axis=-1, keepdims=True)
    i1 = jnp.min(jnp.where(pe == p1, lane, big), axis=-1, keepdims=True)
    pe2 = jnp.where(lane == i1, -1.0, pe)
    p2 = jnp.max(pe2, axis=-1, keepdims=True)
    i2 = jnp.min(jnp.where(pe2 == p2, lane, big), axis=-1, keepdims=True)
    den = p1 + p2
    w1 = pg_top * (p1 / den)
    w2 = pg_top * (p2 / den)
    e1 = (i1 - N_GROUPS).astype(F32)
    e2 = (i2 - N_GROUPS).astype(F32)
    rout_ref[...] = jnp.where(lane == 0, e1, jnp.where(lane == 1, e2, jnp.where(lane == 2, w1, jnp.where(lane == 3, w2, 0.0))))


def _post(mixed, h, wout_bf16, gain, wr_bf16, tm):
    t = h.shape[0]
    row = pl.BlockSpec((tm, D_MODEL), lambda i: (i, 0))
    return pl.pallas_call(
        _post_kernel,
        grid=(t // tm,),
        in_specs=[row, row,
                  pl.BlockSpec((MIX_W, D_MODEL), lambda i: (0, 0)),
                  pl.BlockSpec((1, D_MODEL), lambda i: (0, 0)),
                  pl.BlockSpec((D_MODEL, LANES), lambda i: (0, 0))],
        out_specs=[row, row, pl.BlockSpec((tm, LANES), lambda i: (i, 0))],
        out_shape=[jax.ShapeDtypeStruct((t, D_MODEL), F32), jax.ShapeDtypeStruct((t, D_MODEL), F32),
                   jax.ShapeDtypeStruct((t, LANES), F32)],
        compiler_params=pltpu.CompilerParams(dimension_semantics=("arbitrary",),
                                             vmem_limit_bytes=VMEM_LIMIT),
        name="post",
    )(mixed, h, wout_bf16, gain.reshape(1, D_MODEL), wr_bf16)


def _moe_kernel(bexp_ref, nv_ref, nused_ref, tok_ref, tokn_ref, dst_ref, sw_ref, xn_hbm,
                wg_ref, wu_ref, wd_ref, y_hbm, xbuf, ybuf, wgb, wub, wdb, gsem, ssem):
    i = pl.program_id(0)
    nused = nused_ref[0]
    slot = lax.rem(i, 2)

    def gather_copy(t, r, s):
        return pltpu.make_async_copy(xn_hbm.at[pl.ds(t, 1)], xbuf.at[s, pl.ds(r, 1)], gsem.at[s])

    def scatter_copy(d, r, s):
        return pltpu.make_async_copy(ybuf.at[s, pl.ds(r, 1)], y_hbm.at[pl.ds(d, 1)], ssem.at[s])

    def gather_start(tref, n, s):
        def body(r, carry):
            gather_copy(tref[0, r], r, s).start()
            return carry
        lax.fori_loop(0, n, body, 0)

    def gather_wait(n, s):
        def body(r, carry):
            gather_copy(0, r, s).wait()
            return carry
        lax.fori_loop(0, n, body, 0)

    def scatter_start(n, s):
        def body(r, carry):
            scatter_copy(dst_ref[0, r], r, s).start()
            return carry
        lax.fori_loop(0, n, body, 0)

    def scatter_wait(n, s):
        def body(r, carry):
            scatter_copy(0, r, s).wait()
            return carry
        lax.fori_loop(0, n, body, 0)

    @pl.when(i == 0)
    def _():
        xbuf[...] = jnp.zeros_like(xbuf)
        gather_start(tok_ref, nv_ref[0], 0)

    @pl.when(i < nused)
    def _():
        nv = nv_ref[i]
        gather_wait(nv, slot)

        @pl.when(i + 1 < nused)
        def _():
            gather_start(tokn_ref, nv_ref[i + 1], 1 - slot)

        @pl.when((i == 0) | (bexp_ref[i] != bexp_ref[jnp.maximum(i - 1, 0)]))
        def _():
            wgb[...] = wg_ref[...].astype(BF16)
            wub[...] = wu_ref[...].astype(BF16)
            wdb[...] = wd_ref[...].astype(BF16)

        x = xbuf[slot].astype(BF16)
        hg = jnp.dot(x, wgb[...], preferred_element_type=F32)
        hu = jnp.dot(x, wub[...], preferred_element_type=F32)
        hb = (_silu(hg) * hu).astype(BF16)
        ybuf[slot] = jnp.dot(hb, wdb[...], preferred_element_type=F32) * sw_ref[...]

        @pl.when(i >= 1)
        def _():
            scatter_wait(nv_ref[jnp.maximum(i - 1, 0)], 1 - slot)

        scatter_start(nv, slot)

        @pl.when(i == nused - 1)
        def _():
            scatter_wait(nv, slot)


def _experts(xn, tok, dst, sw, bexp, nv, nused, w_gate, w_up, w_down):
    t = xn.shape[0]
    nb = bexp.shape[0]
    blk = MOE_BLOCK
    smem_blk = lambda f: pl.BlockSpec((None, 1, blk), f, memory_space=pltpu.SMEM)
    grid_spec = pltpu.PrefetchScalarGridSpec(
        num_scalar_prefetch=3,
        grid=(nb,),
        in_specs=[
            smem_blk(lambda i, be, nvr, nu: (i, 0, 0)),
            smem_blk(lambda i, be, nvr, nu: (jnp.minimum(i + 1, nb - 1), 0, 0)),
            smem_blk(lambda i, be, nvr, nu: (i, 0, 0)),
            pl.BlockSpec((None, blk, 1), lambda i, be, nvr, nu: (i, 0, 0)),
            pl.BlockSpec(memory_space=pl.ANY),
            pl.BlockSpec((None, D_MODEL, D_EXPERT), lambda i, be, nvr, nu: (be[i], 0, 0)),
            pl.BlockSpec((None, D_MODEL, D_EXPERT), lambda i, be, nvr, nu: (be[i], 0, 0)),
            pl.BlockSpec((None, D_EXPERT, D_MODEL), lambda i, be, nvr, nu: (be[i], 0, 0)),
        ],
        out_specs=pl.BlockSpec(memory_space=pl.ANY),
        scratch_shapes=[
            pltpu.VMEM((2, blk, D_MODEL), F32),
            pltpu.VMEM((2, blk, D_MODEL), F32),
            pltpu.VMEM((D_MODEL, D_EXPERT), BF16),
            pltpu.VMEM((D_MODEL, D_EXPERT), BF16),
            pltpu.VMEM((D_EXPERT, D_MODEL), BF16),
            pltpu.SemaphoreType.DMA((2,)),
            pltpu.SemaphoreType.DMA((2,)),
        ],
    )
    tok3 = tok.reshape(nb, 1, blk)
    return pl.pallas_call(
        _moe_kernel,
        grid_spec=grid_spec,
        out_shape=jax.ShapeDtypeStruct((TOP_K * t, D_MODEL), F32),
        compiler_params=pltpu.CompilerParams(dimension_semantics=("arbitrary",),
                                             vmem_limit_bytes=VMEM_LIMIT),
        name="experts",
    )(bexp, nv, nused, tok3, tok3, dst.reshape(nb, 1, blk), sw.reshape(nb, blk, 1), xn, w_gate, w_up, w_down)


def _dispatch(rout, t):
    expert = rout[:, 0:TOP_K].astype(jnp.int32)
    weights = rout[:, TOP_K:2 * TOP_K]
    n_assign = t * TOP_K
    e_flat = expert.reshape(n_assign)
    w_flat = weights.reshape(n_assign)
    order = jnp.argsort(e_flat).astype(jnp.int32)
    counts = jnp.zeros((N_EXPERTS,), jnp.int32).at[e_flat].add(1)
    padded = (counts + MOE_BLOCK - 1) // MOE_BLOCK * MOE_BLOCK
    start_raw = jnp.cumsum(counts) - counts
    ends_pad = jnp.cumsum(padded)
    start_pad = ends_pad - padded
    n_blocks = (n_assign + N_EXPERTS * (MOE_BLOCK - 1) + MOE_BLOCK - 1) // MOE_BLOCK
    blk_start = jnp.arange(n_blocks, dtype=jnp.int32) * MOE_BLOCK
    bexp = jnp.minimum(jnp.searchsorted(ends_pad, blk_start, side='right'), N_EXPERTS - 1).astype(jnp.int32)
    nv = jnp.clip(counts[bexp] - (blk_start - start_pad[bexp]), 0, MOE_BLOCK).astype(jnp.int32)
    nused = (ends_pad[-1] // MOE_BLOCK).astype(jnp.int32).reshape(1)
    slot = jnp.arange(n_blocks * MOE_BLOCK, dtype=jnp.int32)
    se = jnp.repeat(bexp, MOE_BLOCK)
    rank = slot - start_pad[se]
    valid = (rank >= 0) & (rank < counts[se])
    a = order[jnp.clip(start_raw[se] + rank, 0, n_assign - 1)]
    tok = jnp.where(valid, a // TOP_K, 0).astype(jnp.int32)
    dst = jnp.where(valid, (a % TOP_K) * t + a // TOP_K, 0).astype(jnp.int32)
    sw = jnp.where(valid, w_flat[a], 0.0).astype(F32)
    return tok, dst, sw, bexp, nv, nused


def _final_kernel(h_ref, ya_ref, yb_ref, g_ref, o_ref):
    x = h_ref[...] + ya_ref[...] + yb_ref[...]
    ms = jnp.mean(x * x, axis=-1, keepdims=True)
    o_ref[...] = x * lax.rsqrt(ms + EPS) * g_ref[...]


def _final(h1, y2, gain, tm):
    t = h1.shape[0]
    row = pl.BlockSpec((tm, D_MODEL), lambda i: (i, 0))
    return pl.pallas_call(
        _final_kernel,
        grid=(t // tm,),
        in_specs=[row,
                  pl.BlockSpec((None, tm, D_MODEL), lambda i: (0, i, 0)),
                  pl.BlockSpec((None, tm, D_MODEL), lambda i: (1, i, 0)),
                  pl.BlockSpec((1, D_MODEL), lambda i: (0, 0))],
        out_specs=row,
        out_shape=jax.ShapeDtypeStruct((t, D_MODEL), F32),
        compiler_params=pltpu.CompilerParams(dimension_semantics=("arbitrary",)),
        name="final_norm",
    )(h1, y2, y2, gain.reshape(1, D_MODEL))


def _relayout_w_in(w):
    a = 4 * RET_W + 3 * GDN_W
    small = w[:, a:a + 2 * GDN_HEADS]
    rest = w[:, a + 2 * GDN_HEADS:]
    pad = jnp.zeros((w.shape[0], LANES - 2 * GDN_HEADS), w.dtype)
    return jnp.concatenate([w[:, :a], rest, small, pad], axis=1)


def kernel(x, w_in, w_out, gdn_conv, gdn_a_log, gdn_dt_bias, ret_norm, gdn_norm, hgrn_norm,
           hgrn_lower_bounds, attn_norm, ffn_norm, router_group, router_expert,
           w_gate, w_up, w_down, final_norm):
    batch, seq, dm = x.shape
    t = batch * seq
    tm = min(512, t)
    lb_all = jnp.cumsum(jax.nn.softmax(hgrn_lower_bounds.astype(F32), axis=0), axis=0)
    lb_all = lb_all - lb_all[0:1]
    consts = _mixer_consts(seq)
    h = x.reshape(t, dm)
    h1 = None
    y2 = None
    for l in range(DEPTH):
        w_l = _relayout_w_in(w_in[l]).astype(BF16)
        if l == 0:
            proj = _inproj(h, None, attn_norm[l], w_l, tm)
        else:
            proj, h = _inproj(h1, y2, attn_norm[l], w_l, tm)
        mixed = _mixer(proj, consts, gdn_conv[l], gdn_a_log[l], gdn_dt_bias[l], ret_norm[l], gdn_norm[l],
                       hgrn_norm[l], lb_all[l], batch, seq)
        wr = jnp.concatenate([router_group[l], router_expert[l],
                              jnp.zeros((dm, LANES - N_GROUPS - N_EXPERTS), F32)], axis=1).astype(BF16)
        h1, xn, rout = _post(mixed, h, w_out[l].astype(BF16), ffn_norm[l], wr, tm)
        tok, dst, sw, bexp, nv, nused = _dispatch(rout, t)
        y2 = _experts(xn, tok, dst, sw, bexp, nv, nused, w_gate[l], w_up[l], w_down[l]).reshape(TOP_K, t, dm)
    out = _final(h1, y2, final_norm, tm)
    return out.reshape(batch, seq, dm)
```

```python
import functools

import jax
import jax.numpy as jnp
import numpy as np
from jax import lax
from jax.experimental import pallas as pl
from jax.experimental.pallas import tpu as pltpu

D_MODEL = 1024
DEPTH = 2
RET_HEADS, RET_HD = 4, 64
GDN_HEADS, GDN_HD = 4, 128
HG_HEADS, HG_HD = 4, 64
RET_W = RET_HEADS * RET_HD
GDN_W = GDN_HEADS * GDN_HD
HG_W = HG_HEADS * HG_HD
MIX_W = RET_W + GDN_W + HG_W
CHUNK = 64
CONV_W = 4
ROPE_BASE = 10000.0
N_GROUPS = 4
EXP_PER_GROUP = 8
N_EXPERTS = N_GROUPS * EXP_PER_GROUP
TOP_K = 2
D_EXPERT = 512
MOE_BLOCK = 256
EPS = 1e-6

LANES = 128
SUBLANES = 8
SUB_BLOCK = 16

W_RET = 0
W_GDN = 4 * RET_W
W_HG = W_GDN + 4 * GDN_W
W_SMALL = W_HG + 4 * HG_W
W_COLS = W_SMALL + LANES

OFF_RET = 0
OFF_GDN = 4 * RET_W
OFF_HG = OFF_GDN + 4 * GDN_W
OFF_SMALL = OFF_HG + 5 * HG_W
PROJ_W = OFF_SMALL + LANES

VMEM_LIMIT = 56 * 1024 * 1024

F32 = jnp.float32
BF16 = jnp.bfloat16


def _mm(a, b):
    return jnp.dot(a.astype(BF16), b.astype(BF16), preferred_element_type=F32)


def _mm_nt(a, b):
    return lax.dot_general(a.astype(BF16), b.astype(BF16), (((1,), (1,)), ((), ())),
                           preferred_element_type=F32)


def _mm_tn(a, b):
    return lax.dot_general(a.astype(BF16), b.astype(BF16), (((0,), (0,)), ((), ())),
                           preferred_element_type=F32)


def _sigmoid(x):
    return jax.nn.sigmoid(x)


def _silu(x):
    return x * _sigmoid(x)


def _softplus(x):
    return jnp.maximum(x, 0.0) + jnp.log(1.0 + jnp.exp(-jnp.abs(x)))


def _group_mean_sq(o, bd_bf16, width):
    x = o * o
    hi = x.astype(BF16)
    lo = (x - hi.astype(F32)).astype(BF16)
    r = jnp.dot(jnp.concatenate([hi, lo], axis=0), bd_bf16, preferred_element_type=F32)
    n = o.shape[0]
    return (r[:n] + r[n:]) * (1.0 / width)


def _rotary(x, cos_t, sin_t):
    lane = lax.broadcasted_iota(jnp.int32, (x.shape[0], LANES), 1)
    first = (lane % RET_HD) < (RET_HD // 2)
    parts = []
    for p in range(x.shape[1] // LANES):
        xp = x[:, LANES * p:LANES * (p + 1)]
        swapped = jnp.where(first, pltpu.roll(xp, LANES - RET_HD // 2, 1), pltpu.roll(xp, RET_HD // 2, 1))
        parts.append(xp * cos_t + swapped * sin_t)
    return jnp.concatenate(parts, axis=1)


def _chunk_cumsum(x, pos):
    d = 1
    while d < CHUNK:
        x = x + jnp.where(pos >= d, pltpu.roll(x, d, 0), 0.0)
        d *= 2
    return x


def _combine(ya_ref, yb_ref, rt_ref):
    rt = rt_ref[...]
    return rt[:, TOP_K:TOP_K + 1] * ya_ref[...] + rt[:, TOP_K + 1:TOP_K + 2] * yb_ref[...]


def _inproj_kernel(*refs, n_add, tiles_per_seq):
    if n_add:
        (h_ref, ya_ref, yb_ref, rt_ref, g_ref, w_ref, cos_ref, sin_ref, convw_ref, alog_ref, dtb_ref, lb_ref,
         proj_ref, st_ref, hout_ref, cbuf) = refs
        x = h_ref[...] + _combine(ya_ref, yb_ref, rt_ref)
        hout_ref[...] = x
    else:
        (h_ref, g_ref, w_ref, cos_ref, sin_ref, convw_ref, alog_ref, dtb_ref, lb_ref,
         proj_ref, st_ref, cbuf) = refs
        x = h_ref[...]
    tm = x.shape[0]
    ms = jnp.mean(x * x, axis=-1, keepdims=True)
    xb = (x * lax.rsqrt(ms + EPS) * g_ref[...]).astype(BF16)

    def mm(lo, hi):
        return jnp.dot(xb, w_ref[:, lo:hi], preferred_element_type=F32)

    pos = lax.broadcasted_iota(jnp.int32, (tm, 1), 0) % CHUNK

    pr = mm(W_RET, W_RET + 4 * RET_W)
    cos_t = cos_ref[...]
    sin_t = sin_ref[...]
    q = pr[:, 0:RET_W]
    k = pr[:, RET_W:2 * RET_W]
    proj_ref[:, OFF_RET:OFF_RET + RET_W] = _rotary(q, cos_t, sin_t)
    proj_ref[:, OFF_RET + RET_W:OFF_RET + 2 * RET_W] = _rotary(k, cos_t, sin_t) * (RET_HD ** -0.5)
    proj_ref[:, OFF_RET + 2 * RET_W:OFF_RET + 3 * RET_W] = pr[:, 2 * RET_W:3 * RET_W]
    proj_ref[:, OFF_RET + 3 * RET_W:OFF_RET + 4 * RET_W] = _silu(pr[:, 3 * RET_W:4 * RET_W])

    first = (pl.program_id(0) % tiles_per_seq) == 0

    @pl.when(first)
    def _():
        cbuf[0:SUBLANES, :] = jnp.zeros((SUBLANES, 3 * GDN_W), F32)

    @pl.when(jnp.logical_not(first))
    def _():
        cbuf[0:SUBLANES, :] = cbuf[tm:tm + SUBLANES, :]

    cbuf[SUBLANES:SUBLANES + tm, :] = mm(W_GDN, W_GDN + 3 * GDN_W)
    y = cbuf[SUBLANES:SUBLANES + tm, :] * convw_ref[CONV_W - 1:CONV_W, :]
    for j in range(CONV_W - 1):
        lo = SUBLANES - (CONV_W - 1) + j
        y = y + cbuf[lo:lo + tm, :] * convw_ref[j:j + 1, :]
    y = _silu(y)
    for h in range(GDN_HEADS):
        cq = y[:, GDN_HD * h:GDN_HD * (h + 1)]
        ck = y[:, GDN_W + GDN_HD * h:GDN_W + GDN_HD * (h + 1)]
        proj_ref[:, OFF_GDN + GDN_HD * h:OFF_GDN + GDN_HD * (h + 1)] = (
            cq * (lax.rsqrt(jnp.sum(cq * cq, axis=-1, keepdims=True) + EPS) * (GDN_HD ** -0.5)))
        proj_ref[:, OFF_GDN + GDN_W + GDN_HD * h:OFF_GDN + GDN_W + GDN_HD * (h + 1)] = (
            ck * lax.rsqrt(jnp.sum(ck * ck, axis=-1, keepdims=True) + EPS))
    proj_ref[:, OFF_GDN + 2 * GDN_W:OFF_GDN + 3 * GDN_W] = y[:, 2 * GDN_W:3 * GDN_W]
    proj_ref[:, OFF_GDN + 3 * GDN_W:OFF_GDN + 4 * GDN_W] = _silu(mm(W_GDN + 3 * GDN_W, W_GDN + 4 * GDN_W))

    ph = mm(W_HG, W_HG + 4 * HG_W)
    lb = lb_ref[...]
    f = lb + (1.0 - lb) * _sigmoid(ph[:, HG_W:2 * HG_W])
    proj_ref[:, OFF_HG:OFF_HG + HG_W] = _silu(ph[:, 0:HG_W])
    proj_ref[:, OFF_HG + HG_W:OFF_HG + 2 * HG_W] = 1.0 - f
    proj_ref[:, OFF_HG + 2 * HG_W:OFF_HG + 3 * HG_W] = ph[:, 2 * HG_W:3 * HG_W]
    proj_ref[:, OFF_HG + 3 * HG_W:OFF_HG + 4 * HG_W] = _silu(ph[:, 3 * HG_W:4 * HG_W])
    proj_ref[:, OFF_HG + 4 * HG_W:OFF_HG + 5 * HG_W] = _chunk_cumsum(jnp.log(f), pos)

    small = mm(W_SMALL, W_SMALL + LANES)
    gcum = _chunk_cumsum(-jnp.exp(alog_ref[...]) * _softplus(small + dtb_ref[...]), pos)
    lane = lax.broadcasted_iota(jnp.int32, (tm, LANES), 1)
    proj_ref[:, OFF_SMALL:OFF_SMALL + LANES] = jnp.where(lane < GDN_HEADS, gcum, _sigmoid(small))
    gt = gcum.T
    for c in range(tm // CHUNK):
        st_ref[c] = gt[0:SUBLANES, CHUNK * c:CHUNK * (c + 1)]


def _inproj(h, y2, rout, gain, w_bf16, consts, conv_w, a_log, dt_bias, lb, seq, tm):
    t = h.shape[0]
    n_add = y2 is not None
    tps = seq // tm
    cos_t, sin_t = consts[0], consts[1]
    pad4 = lambda a: jnp.concatenate([a.astype(F32), jnp.zeros((LANES - GDN_HEADS,), F32)]).reshape(1, LANES)
    const = lambda shape: pl.BlockSpec(shape, lambda i: (0,) * len(shape))
    row = pl.BlockSpec((tm, D_MODEL), lambda i: (i, 0))
    in_specs = [row]
    args = [h]
    if n_add:
        in_specs += [pl.BlockSpec((tm, D_MODEL), lambda i: (i, 0)),
                     pl.BlockSpec((tm, D_MODEL), lambda i: (t // tm + i, 0)),
                     pl.BlockSpec((tm, LANES), lambda i: (i, 0))]
        args += [y2, y2, rout]
    in_specs += [const((1, D_MODEL)),
                 pl.BlockSpec((D_MODEL, W_COLS), lambda i: (0, 0), pipeline_mode=pl.Buffered(1)),
                 pl.BlockSpec((tm, LANES), lambda i: (i % tps, 0)),
                 pl.BlockSpec((tm, LANES), lambda i: (i % tps, 0)),
                 const((CONV_W, 3 * GDN_W)), const((1, LANES)), const((1, LANES)), const((1, HG_W))]
    args += [gain.reshape(1, D_MODEL), w_bf16, cos_t, sin_t, conv_w.astype(F32), pad4(a_log), pad4(dt_bias),
             lb.reshape(1, HG_W).astype(F32)]
    out_shape = [jax.ShapeDtypeStruct((t, PROJ_W), F32),
                 jax.ShapeDtypeStruct((t // CHUNK, SUBLANES, CHUNK), F32)]
    out_specs = [pl.BlockSpec((tm, PROJ_W), lambda i: (i, 0)),
                 pl.BlockSpec((tm // CHUNK, SUBLANES, CHUNK), lambda i: (i, 0, 0))]
    if n_add:
        out_shape.append(jax.ShapeDtypeStruct((t, D_MODEL), F32))
        out_specs.append(row)
    res = pl.pallas_call(
        functools.partial(_inproj_kernel, n_add=n_add, tiles_per_seq=tps),
        grid=(t // tm,),
        in_specs=in_specs,
        out_specs=out_specs,
        out_shape=out_shape,
        scratch_shapes=[pltpu.VMEM((tm + SUBLANES, 3 * GDN_W), F32)],
        compiler_params=pltpu.CompilerParams(dimension_semantics=("arbitrary",),
                                             vmem_limit_bytes=VMEM_LIMIT),
        name="inproj",
    )(*args)
    return res


def _mixer_kernel(proj_ref, st_ref, dmat_ref, xi_ref, zeta_ref, rdec_ref, bd_ref, bdb_ref,
                  retg_ref, gdng_ref, hgg_ref, out_ref, ret_s, gdn_s, hg_s):
    @pl.when(pl.program_id(0) == 0)
    def _():
        ret_s[...] = jnp.zeros_like(ret_s)
        gdn_s[...] = jnp.zeros_like(gdn_s)
        hg_s[...] = jnp.zeros_like(hg_s)

    gens = []
    for b in range(proj_ref.shape[0]):
        gens.append(_gdn_chunk(proj_ref.at[b], st_ref.at[b], out_ref.at[b], gdn_s.at[b], gdng_ref))
        gens.append(_hgrn_chunk(proj_ref.at[b], out_ref.at[b], hg_s.at[b], bd_ref, bdb_ref, hgg_ref))
        gens.append(_ret_chunk(proj_ref.at[b], out_ref.at[b], ret_s.at[b], dmat_ref, xi_ref, zeta_ref, rdec_ref,
                               bd_ref, bdb_ref, retg_ref))
    while gens:
        alive = []
        for g in gens:
            try:
                next(g)
                alive.append(g)
            except StopIteration:
                pass
        gens = alive


def _ret_chunk(proj_ref, out_ref, ret_s, dmat_ref, xi_ref, zeta_ref, rdec_ref, bd_ref, bdb_ref, retg_ref):
    c = CHUNK
    head4 = lax.broadcasted_iota(jnp.int32, (c, RET_W), 1) // RET_HD
    qr = proj_ref[:, OFF_RET:OFF_RET + RET_W]
    kr = proj_ref[:, OFF_RET + RET_W:OFF_RET + 2 * RET_W]
    v = proj_ref[:, OFF_RET + 2 * RET_W:OFF_RET + 3 * RET_W]
    s_ret = ret_s[...]
    o = _mm(qr * xi_ref[...], s_ret)
    scs = [_mm_nt(jnp.where(head4 == h, qr, 0.0), kr) * dmat_ref[h] for h in range(RET_HEADS)]
    yield
    ohs = [_mm(sc, v) for sc in scs]
    ret_s[...] = s_ret * rdec_ref[...] + bd_ref[...] * _mm_tn(kr * zeta_ref[...], v)
    yield
    for h in range(RET_HEADS):
        o = o + jnp.where(head4 == h, ohs[h], 0.0)
    ms = _group_mean_sq(o, bdb_ref[...], RET_HD)
    yield
    out_ref[:, 0:RET_W] = (o * lax.rsqrt(ms + EPS) * retg_ref[...]
                           * proj_ref[:, OFF_RET + 3 * RET_W:OFF_RET + 4 * RET_W])


def _gdn_chunk(proj_ref, st_ref, out_ref, gdn_s, gdng_ref):
    c = CHUNK
    nh = GDN_HEADS
    hs = range(nh)
    r64 = lax.broadcasted_iota(jnp.int32, (c, c), 0)
    c64 = lax.broadcasted_iota(jnp.int32, (c, c), 1)
    causal = r64 >= c64
    strict = r64 > c64
    eye = (r64 == c64).astype(F32)
    qn = [proj_ref[:, OFF_GDN + GDN_HD * h:OFF_GDN + GDN_HD * (h + 1)] for h in hs]
    kn = [proj_ref[:, OFF_GDN + GDN_W + GDN_HD * h:OFF_GDN + GDN_W + GDN_HD * (h + 1)] for h in hs]
    small = proj_ref[:, OFF_SMALL:OFF_SMALL + LANES]
    gcol = [small[:, h:h + 1] for h in hs]
    bcol = [small[:, nh + h:nh + h + 1] for h in hs]
    gamma = [jnp.where(causal, jnp.exp(jnp.minimum(gcol[h] - st_ref[h:h + 1, :], 0.0)), 0.0) for h in hs]
    kk = [_mm_nt(kn[h], kn[h]) for h in hs]
    qk = [_mm_nt(qn[h], kn[h]) * gamma[h] for h in hs]
    yield
    p = [jnp.where(strict, bcol[h] * kk[h] * gamma[h], 0.0) for h in hs]
    t_inv = [eye - p[h] for h in hs]
    eg = [jnp.exp(gcol[h]) for h in hs]
    cv = [proj_ref[:, OFF_GDN + 2 * GDN_W + GDN_HD * h:OFF_GDN + 2 * GDN_W + GDN_HD * (h + 1)] for h in hs]
    rhs = [jnp.concatenate([cv[h] * bcol[h], kn[h] * (bcol[h] * eg[h])], axis=1) for h in hs]
    glast = [small[c - 1:c, h:h + 1] for h in hs]
    kd = [kn[h] * jnp.exp(glast[h] - gcol[h]) for h in hs]
    qd = [qn[h] * eg[h] for h in hs]
    for _ in range(5):
        p = [_mm(x, x) for x in p]
        yield
        t_inv = [t_inv[h] + _mm(t_inv[h], p[h]) for h in hs]
        yield
    sol = [_mm(t_inv[h], rhs[h]) for h in hs]
    yield
    s_g = [gdn_s[h] for h in hs]
    v_new = [sol[h][:, :GDN_HD] - _mm(sol[h][:, GDN_HD:], s_g[h]) for h in hs]
    oh = [_mm(qd[h], s_g[h]) for h in hs]
    yield
    oh = [oh[h] + _mm(qk[h], v_new[h]) for h in hs]
    for h in hs:
        gdn_s[h] = s_g[h] * jnp.exp(glast[h]) + _mm_tn(kd[h], v_new[h])
    yield
    for h in hs:
        msd = jnp.mean(oh[h] * oh[h], axis=-1, keepdims=True)
        dzs = proj_ref[:, OFF_GDN + 3 * GDN_W + GDN_HD * h:OFF_GDN + 3 * GDN_W + GDN_HD * (h + 1)]
        out_ref[:, RET_W + GDN_HD * h:RET_W + GDN_HD * (h + 1)] = oh[h] * lax.rsqrt(msd + EPS) * gdng_ref[...] * dzs


def _hgrn_chunk(proj_ref, out_ref, hg_s, bd_ref, bdb_ref, hgg_ref):
    c = CHUNK
    head4 = lax.broadcasted_iota(jnp.int32, (c, HG_W), 1) // HG_HD
    qh = proj_ref[:, OFF_HG:OFF_HG + HG_W]
    kh = proj_ref[:, OFF_HG + HG_W:OFF_HG + 2 * HG_W]
    hv = proj_ref[:, OFF_HG + 2 * HG_W:OFF_HG + 3 * HG_W]
    gl = proj_ref[:, OFF_HG + 4 * HG_W:OFF_HG + 5 * HG_W]
    row = lax.broadcasted_iota(jnp.int32, (c, HG_W), 0)
    blk = row // SUB_BLOCK
    rin = row % SUB_BLOCK
    nblk = c // SUB_BLOCK
    s_h = hg_s[...]
    o = _mm_nt(qh * jnp.exp(gl), s_h)
    glast = gl[c - 1:c, :]
    hg_s[...] = s_h * jnp.exp(glast) + bd_ref[...] * _mm_tn(hv, kh * jnp.exp(glast - gl))
    yield
    gref = gl[SUB_BLOCK - 1:SUB_BLOCK, :]
    for i in range(2, nblk):
        gref = jnp.where(blk >= i, gl[SUB_BLOCK * i - 1:SUB_BLOCK * i, :], gref)
    qs = qh * jnp.exp(jnp.minimum(gl - gref, 0.0))
    ks = []
    for i in range(1, nblk):
        gi = gl[SUB_BLOCK * i - 1:SUB_BLOCK * i, :]
        ks.append(jnp.where(row < SUB_BLOCK * i, kh * jnp.exp(jnp.minimum(gi - gl, 0.0)), 0.0))
    kcat = jnp.concatenate(ks, axis=1)
    yield
    atts = []
    for h in range(HG_HEADS):
        mh = head4 == h
        qcat = jnp.concatenate([jnp.where(mh & (blk == i), qs, 0.0) for i in range(1, nblk)], axis=1)
        atts.append(_mm_nt(qcat, kcat))
    yield
    offs = [_mm(a, hv) for a in atts]
    yield
    for h in range(HG_HEADS):
        o = o + jnp.where(head4 == h, offs[h], 0.0)
    ps = [qh * kh]
    for j in range(1, SUB_BLOCK):
        e = jnp.exp(jnp.minimum(gl - pltpu.roll(gl, j, 0), 0.0))
        ps.append(jnp.where(rin >= j, qh * e * pltpu.roll(kh, j, 0), 0.0))
        if j % 4 == 0:
            yield
    acat = jnp.dot(jnp.concatenate(ps, axis=0).astype(BF16), bdb_ref[...], preferred_element_type=F32)
    yield
    o = o + acat[0:c] * hv
    for j in range(1, SUB_BLOCK):
        o = o + acat[c * j:c * (j + 1)] * pltpu.roll(hv, j, 0)
        if j % 4 == 0:
            yield
    ms = _group_mean_sq(o, bdb_ref[...], HG_HD)
    yield
    out_ref[:, RET_W + GDN_W:MIX_W] = (o * lax.rsqrt(ms + EPS) * hgg_ref[...]
                                      * proj_ref[:, OFF_HG + 3 * HG_W:OFF_HG + 4 * HG_W])


def _mixer_consts(seq):
    with jax.ensure_compile_time_eval():
        return _mixer_consts_eval(seq)


def _mixer_consts_eval(seq):
    hd = RET_HD
    inv = ROPE_BASE ** (-jnp.arange(0, hd, 2, dtype=F32) / hd)
    ang = jnp.arange(seq, dtype=F32)[:, None] * inv[None, :]
    cos = jnp.cos(ang)
    sin = jnp.sin(ang)
    cos_t = jnp.tile(jnp.concatenate([cos, cos], axis=1), (1, LANES // RET_HD))
    sin_t = jnp.tile(jnp.concatenate([-sin, sin], axis=1), (1, LANES // RET_HD))
    log_gamma = jnp.log(1.0 - 2.0 ** (-5.0 - jnp.arange(RET_HEADS, dtype=F32)))
    idx = jnp.arange(CHUNK, dtype=F32)
    diff = idx[:, None] - idx[None, :]
    cm = diff >= 0
    dmat = jnp.where(cm, jnp.exp(log_gamma[:, None, None] * jnp.where(cm, diff, 0.0)), 0.0)
    zeta = jnp.exp(log_gamma[:, None] * (CHUNK - 1.0 - idx))
    xi = jnp.exp(log_gamma[:, None] * (idx + 1.0))
    zeta_t = jnp.repeat(zeta.T, RET_HD, axis=1)
    xi_t = jnp.repeat(xi.T, RET_HD, axis=1)
    rdec = jnp.repeat(jnp.exp(log_gamma * CHUNK), RET_HD)[None, :]
    hid = np.arange(RET_W) // RET_HD
    bd = jnp.asarray((hid[:, None] == hid[None, :]).astype(np.float32))
    return cos_t, sin_t, dmat, xi_t, zeta_t, rdec, bd, bd.astype(BF16)


def _mixer(proj, st, consts, ret_g, gdn_g, hg_g, batch, seq):
    _, _, dmat, xi_t, zeta_t, rdec, bd, bd_bf16 = consts
    n = seq // CHUNK
    const = lambda shape: pl.BlockSpec(shape, lambda i: (0,) * len(shape))
    in_specs = [
        pl.BlockSpec((batch, CHUNK, PROJ_W), lambda i: (0, i, 0)),
        pl.BlockSpec((batch, None, SUBLANES, CHUNK), lambda i: (0, i, 0, 0)),
        const((RET_HEADS, CHUNK, CHUNK)),
        const((CHUNK, RET_W)),
        const((CHUNK, RET_W)),
        const((1, RET_W)),
        const((RET_W, RET_W)),
        const((RET_W, RET_W)),
        const((1, RET_W)),
        const((1, GDN_HD)),
        const((1, HG_W)),
    ]
    return pl.pallas_call(
        _mixer_kernel,
        grid=(n,),
        in_specs=in_specs,
        out_specs=pl.BlockSpec((batch, CHUNK, MIX_W), lambda i: (0, i, 0)),
        out_shape=jax.ShapeDtypeStruct((batch, seq, MIX_W), F32),
        scratch_shapes=[
            pltpu.VMEM((batch, RET_W, RET_W), F32),
            pltpu.VMEM((batch, GDN_HEADS, GDN_HD, GDN_HD), F32),
            pltpu.VMEM((batch, HG_W, HG_W), F32),
        ],
        compiler_params=pltpu.CompilerParams(dimension_semantics=("arbitrary",),
                                             vmem_limit_bytes=VMEM_LIMIT),
        name="mixer",
    )(proj.reshape(batch, seq, PROJ_W), st.reshape(batch, n, SUBLANES, CHUNK), dmat, xi_t, zeta_t, rdec, bd,
      bd_bf16, ret_g.reshape(1, RET_W).astype(F32), gdn_g.reshape(1, GDN_HD).astype(F32),
      hg_g.reshape(1, HG_W).astype(F32))


def _post_kernel(mixed_ref, h_ref, wout_ref, g_ref, wr_ref, h1_ref, xn_ref, rout_ref):
    h1 = h_ref[...] + jnp.dot(mixed_ref[...].astype(BF16), wout_ref[...], preferred_element_type=F32)
    h1_ref[...] = h1
    ms = jnp.mean(h1 * h1, axis=-1, keepdims=True)
    xn = h1 * lax.rsqrt(ms + EPS) * g_ref[...]
    xn_ref[...] = xn
    logits = jnp.dot(xn.astype(BF16), wr_ref[...], preferred_element_type=F32)
    lane = lax.broadcasted_iota(jnp.int32, logits.shape, 1)
    big = jnp.int32(1 << 20)
    ninf = jnp.float32(-jnp.inf)
    gmask = lane < N_GROUPS
    gl = jnp.where(gmask, logits, ninf)
    eg = jnp.exp(gl - jnp.max(gl, axis=-1, keepdims=True))
    pg = eg / jnp.sum(eg, axis=-1, keepdims=True)
    pg_top = jnp.max(pg, axis=-1, keepdims=True)
    grp = jnp.min(jnp.where(gmask & (pg == pg_top), lane, big), axis=-1, keepdims=True)
    lo = N_GROUPS + EXP_PER_GROUP * grp
    emask = (lane >= lo) & (lane < lo + EXP_PER_GROUP)
    el = jnp.where(emask, logits, ninf)
    ee = jnp.exp(el - jnp.max(el, axis=-1, keepdims=True))
    pe = ee / jnp.sum(ee, axis=-1, keepdims=True)
    pe = jnp.where(emask, pe, -1.0)
    p1 = jnp.max(pe, axis=-1, keepdims=True)
    i1 = jnp.min(jnp.where(pe == p1, lane, big), axis=-1, keepdims=True)
    pe2 = jnp.where(lane == i1, -1.0, pe)
    p2 = jnp.max(pe2, axis=-1, keepdims=True)
    i2 = jnp.min(jnp.where(pe2 == p2, lane, big), axis=-1, keepdims=True)
    den = p1 + p2
    w1 = pg_top * (p1 / den)
    w2 = pg_top * (p2 / den)
    e1 = (i1 - N_GROUPS).astype(F32)
    e2 = (i2 - N_GROUPS).astype(F32)
    rout_ref[...] = jnp.where(lane == 0, e1, jnp.where(lane == 1, e2, jnp.where(lane == 2, w1, jnp.where(lane == 3, w2, 0.0))))


def _post(mixed, h, wout_bf16, gain, wr_bf16, tm):
    t = h.shape[0]
    row = pl.BlockSpec((tm, D_MODEL), lambda i: (i, 0))
    return pl.pallas_call(
        _post_kernel,
        grid=(t // tm,),
        in_specs=[row, row,
                  pl.BlockSpec((MIX_W, D_MODEL), lambda i: (0, 0)),
                  pl.BlockSpec((1, D_MODEL), lambda i: (0, 0)),
                  pl.BlockSpec((D_MODEL, LANES), lambda i: (0, 0))],
        out_specs=[row, row, pl.BlockSpec((tm, LANES), lambda i: (i, 0))],
        out_shape=[jax.ShapeDtypeStruct((t, D_MODEL), F32), jax.ShapeDtypeStruct((t, D_MODEL), F32),
                   jax.ShapeDtypeStruct((t, LANES), F32)],
        compiler_params=pltpu.CompilerParams(dimension_semantics=("arbitrary",),
                                             vmem_limit_bytes=VMEM_LIMIT),
        name="post",
    )(mixed, h, wout_bf16, gain.reshape(1, D_MODEL), wr_bf16)


def _moe_kernel(bexp_ref, tok_ref, tokn_ref, dstp_ref, dst_ref, xn_hbm,
                wg_ref, wu_ref, wd_ref, y_hbm, xbuf, ybuf, wgb, wub, wdb, gsem, ssem, *, dummy_row):
    i = pl.program_id(0)
    nb = pl.num_programs(0)
    blk = MOE_BLOCK
    row_c = lax.rem(i, SUBLANES)
    row_n = lax.rem(jnp.minimum(i + 1, nb - 1), SUBLANES)
    row_p = lax.rem(jnp.maximum(i - 1, 0), SUBLANES)
    n_piece = D_EXPERT // LANES
    per = blk // n_piece

    def gather_copy(t, r, s):
        return pltpu.make_async_copy(xn_hbm.at[pl.ds(t, 1)], xbuf.at[s, pl.ds(r, 1)], gsem.at[s])

    def scatter_copy(d, r, s):
        return pltpu.make_async_copy(ybuf.at[s, pl.ds(r, 1)], y_hbm.at[pl.ds(d, 1)], ssem.at[s])

    @pl.when(i == 0)
    def _():
        ybuf[1] = jnp.zeros((blk, D_MODEL), F32)
        for r in range(blk):
            gather_copy(tok_ref[row_c, r], r, 0).start()

    @pl.when((i == 0) | (bexp_ref[i] != bexp_ref[jnp.maximum(i - 1, 0)]))
    def _():
        wgb[...] = wg_ref[...].astype(BF16)
        wub[...] = wu_ref[...].astype(BF16)
        wdb[...] = wd_ref[...].astype(BF16)

    def step(s):
        for r in range(blk):
            gather_copy(0, r, s).wait()
        first = i == 0
        x = xbuf[s].astype(BF16)
        parts = []
        for c in range(n_piece):
            cs = slice(LANES * c, LANES * (c + 1))
            hg = jnp.dot(x, wgb[:, cs], preferred_element_type=F32)
            hu = jnp.dot(x, wub[:, cs], preferred_element_type=F32)
            parts.append((_silu(hg) * hu).astype(BF16))
            for r in range(per * c, per * (c + 1)):
                gather_copy(tokn_ref[row_n, r], r, 1 - s).start()
                scatter_copy(jnp.where(first, dummy_row + r, dstp_ref[row_p, r]), r, 1 - s).start()
        hb = jnp.concatenate(parts, axis=1)
        ybuf[s] = jnp.dot(hb, wdb[...], preferred_element_type=F32)
        for r in range(blk):
            scatter_copy(0, r, 1 - s).wait()

        @pl.when(i == nb - 1)
        def _():
            for r in range(blk):
                scatter_copy(dst_ref[row_c, r], r, s).start()
            for r in range(blk):
                scatter_copy(0, r, s).wait()
            for r in range(blk):
                gather_copy(0, r, 1 - s).wait()

    @pl.when(lax.rem(i, 2) == 0)
    def _():
        step(0)

    @pl.when(lax.rem(i, 2) == 1)
    def _():
        step(1)


def _experts(xn, tok, dst, bexp, w_gate, w_up, w_down):
    t = xn.shape[0]
    nb = bexp.shape[0]
    blk = MOE_BLOCK
    smem_blk = lambda f: pl.BlockSpec((SUBLANES, blk), f, memory_space=pltpu.SMEM)
    grid_spec = pltpu.PrefetchScalarGridSpec(
        num_scalar_prefetch=1,
        grid=(nb,),
        in_specs=[
            smem_blk(lambda i, be: (i // SUBLANES, 0)),
            smem_blk(lambda i, be: (jnp.minimum(i + 1, nb - 1) // SUBLANES, 0)),
            smem_blk(lambda i, be: (jnp.maximum(i - 1, 0) // SUBLANES, 0)),
            smem_blk(lambda i, be: (i // SUBLANES, 0)),
            pl.BlockSpec(memory_space=pl.ANY),
            pl.BlockSpec((None, D_MODEL, D_EXPERT), lambda i, be: (be[i], 0, 0)),
            pl.BlockSpec((None, D_MODEL, D_EXPERT), lambda i, be: (be[i], 0, 0)),
            pl.BlockSpec((None, D_EXPERT, D_MODEL), lambda i, be: (be[i], 0, 0)),
        ],
        out_specs=pl.BlockSpec(memory_space=pl.ANY),
        scratch_shapes=[
            pltpu.VMEM((2, blk, D_MODEL), F32),
            pltpu.VMEM((2, blk, D_MODEL), F32),
            pltpu.VMEM((D_MODEL, D_EXPERT), BF16),
            pltpu.VMEM((D_MODEL, D_EXPERT), BF16),
            pltpu.VMEM((D_EXPERT, D_MODEL), BF16),
            pltpu.SemaphoreType.DMA((2,)),
            pltpu.SemaphoreType.DMA((2,)),
        ],
    )
    tok2 = tok.reshape(nb, blk)
    dst2 = dst.reshape(nb, blk)
    return pl.pallas_call(
        functools.partial(_moe_kernel, dummy_row=TOP_K * t),
        grid_spec=grid_spec,
        out_shape=jax.ShapeDtypeStruct((TOP_K * t + blk, D_MODEL), F32),
        compiler_params=pltpu.CompilerParams(dimension_semantics=("arbitrary",),
                                             vmem_limit_bytes=VMEM_LIMIT),
        name="experts",
    )(bexp, tok2, tok2, dst2, dst2, xn, w_gate, w_up, w_down)


def _dispatch(rout, t):
    n_assign = t * TOP_K
    e_flat = rout[:, 0:TOP_K].astype(jnp.int32).reshape(n_assign)
    e_sorted, order = lax.sort_key_val(e_flat, jnp.arange(n_assign, dtype=jnp.int32))
    start_raw = jnp.searchsorted(e_sorted, jnp.arange(N_EXPERTS + 1, dtype=jnp.int32), side='left').astype(jnp.int32)
    counts = start_raw[1:] - start_raw[:-1]
    padded = (counts + MOE_BLOCK - 1) // MOE_BLOCK * MOE_BLOCK
    ends_pad = jnp.cumsum(padded)
    start_pad = ends_pad - padded
    n_blocks = (n_assign + N_EXPERTS * (MOE_BLOCK - 1) + MOE_BLOCK - 1) // MOE_BLOCK
    n_blocks = (n_blocks + SUBLANES - 1) // SUBLANES * SUBLANES
    blk_start = jnp.arange(n_blocks, dtype=jnp.int32) * MOE_BLOCK
    bexp = jnp.minimum(jnp.searchsorted(ends_pad, blk_start, side='right'), N_EXPERTS - 1).astype(jnp.int32)
    slot = jnp.arange(n_blocks * MOE_BLOCK, dtype=jnp.int32)
    se = jnp.repeat(bexp, MOE_BLOCK)
    rank = slot - start_pad[se]
    valid = (rank >= 0) & (rank < counts[se])
    a = order[jnp.clip(start_raw[se] + rank, 0, n_assign - 1)]
    tok = jnp.where(valid, a // TOP_K, 0).astype(jnp.int32)
    dst = jnp.where(valid, (a % TOP_K) * t + a // TOP_K, TOP_K * t + slot % MOE_BLOCK).astype(jnp.int32)
    return tok, dst, bexp


def _final_kernel(h_ref, ya_ref, yb_ref, rt_ref, g_ref, o_ref):
    x = h_ref[...] + _combine(ya_ref, yb_ref, rt_ref)
    ms = jnp.mean(x * x, axis=-1, keepdims=True)
    o_ref[...] = x * lax.rsqrt(ms + EPS) * g_ref[...]


def _final(h1, y2, rout, gain, tm):
    t = h1.shape[0]
    row = pl.BlockSpec((tm, D_MODEL), lambda i: (i, 0))
    return pl.pallas_call(
        _final_kernel,
        grid=(t // tm,),
        in_specs=[row, row,
                  pl.BlockSpec((tm, D_MODEL), lambda i: (t // tm + i, 0)),
                  pl.BlockSpec((tm, LANES), lambda i: (i, 0)),
                  pl.BlockSpec((1, D_MODEL), lambda i: (0, 0))],
        out_specs=row,
        out_shape=jax.ShapeDtypeStruct((t, D_MODEL), F32),
        compiler_params=pltpu.CompilerParams(dimension_semantics=("arbitrary",)),
        name="final_norm",
    )(h1, y2, y2, rout, gain.reshape(1, D_MODEL))


def _prep_w_kernel(w_ref, o_ref):
    a = 4 * RET_W + 3 * GDN_W
    n_small = 2 * GDN_HEADS
    rest = w_ref.shape[1] - a - n_small
    o_ref[:, 0:a] = w_ref[:, 0:a].astype(BF16)
    o_ref[:, a:a + rest] = w_ref[:, a + n_small:a + n_small + rest].astype(BF16)
    lane = lax.broadcasted_iota(jnp.int32, (w_ref.shape[0], LANES), 1)
    o_ref[:, W_SMALL:W_SMALL + LANES] = jnp.where(lane < n_small, w_ref[:, a:a + LANES], 0.0).astype(BF16)


def _prep_w_in(w):
    rows = 128
    return pl.pallas_call(
        _prep_w_kernel,
        grid=(w.shape[0] // rows,),
        in_specs=[pl.BlockSpec((rows, w.shape[1]), lambda i: (i, 0))],
        out_specs=pl.BlockSpec((rows, W_COLS), lambda i: (i, 0)),
        out_shape=jax.ShapeDtypeStruct((w.shape[0], W_COLS), BF16),
        name="prep_w_in",
    )(w)


def kernel(x, w_in, w_out, gdn_conv, gdn_a_log, gdn_dt_bias, ret_norm, gdn_norm, hgrn_norm,
           hgrn_lower_bounds, attn_norm, ffn_norm, router_group, router_expert,
           w_gate, w_up, w_down, final_norm):
    batch, seq, dm = x.shape
    t = batch * seq
    tm = min(512, seq)
    lb_all = jnp.cumsum(jax.nn.softmax(hgrn_lower_bounds.astype(F32), axis=0), axis=0)
    lb_all = lb_all - lb_all[0:1]
    consts = _mixer_consts(seq)
    h = x.reshape(t, dm)
    h1 = None
    y2 = None
    rout = None
    for l in range(DEPTH):
        res = _inproj(h if l == 0 else h1, y2, rout, attn_norm[l], _prep_w_in(w_in[l]), consts, gdn_conv[l],
                      gdn_a_log[l], gdn_dt_bias[l], lb_all[l], seq, tm)
        if l == 0:
            proj, st = res
        else:
            proj, st, h = res
        mixed = _mixer(proj, st, consts, ret_norm[l], gdn_norm[l], hgrn_norm[l], batch, seq).reshape(t, MIX_W)
        wr = jnp.concatenate([router_group[l], router_expert[l],
                              jnp.zeros((dm, LANES - N_GROUPS - N_EXPERTS), F32)], axis=1).astype(BF16)
        h1, xn, rout = _post(mixed, h, w_out[l].astype(BF16), ffn_norm[l], wr, tm)
        tok, dst, bexp = _dispatch(rout, t)
        y2 = _experts(xn, tok, dst, bexp, w_gate[l], w_up[l], w_down[l])
    out = _final(h1, y2, rout, final_norm, tm)
    return out.reshape(batch, seq, dm)
```

```python
import functools

import jax
import jax.numpy as jnp
import numpy as np
from jax import lax
from jax.experimental import pallas as pl
from jax.experimental.pallas import tpu as pltpu

D_MODEL = 1024
DEPTH = 2
RET_HEADS, RET_HD = 4, 64
GDN_HEADS, GDN_HD = 4, 128
HG_HEADS, HG_HD = 4, 64
RET_W = RET_HEADS * RET_HD
GDN_W = GDN_HEADS * GDN_HD
HG_W = HG_HEADS * HG_HD
MIX_W = RET_W + GDN_W + HG_W
CHUNK = 64
CONV_W = 4
ROPE_BASE = 10000.0
N_GROUPS = 4
EXP_PER_GROUP = 8
N_EXPERTS = N_GROUPS * EXP_PER_GROUP
TOP_K = 2
D_EXPERT = 512
MOE_BLOCK = 256
EPS = 1e-6

LANES = 128
SUBLANES = 8
ROW_TILES = D_MODEL // LANES
MIX_TILE = 128
SUB_BLOCK = 16

W_RET = 0
W_GDN = 4 * RET_W
W_HG = W_GDN + 4 * GDN_W
W_SMALL = W_HG + 4 * HG_W
W_COLS = W_SMALL + LANES

OFF_RET = 0
OFF_GDN = 4 * RET_W
OFF_HG = OFF_GDN + 4 * GDN_W
OFF_SMALL = OFF_HG + 5 * HG_W
PROJ_W = OFF_SMALL + LANES

VMEM_LIMIT = 56 * 1024 * 1024

F32 = jnp.float32
BF16 = jnp.bfloat16


def _mm(a, b):
    return jnp.dot(a.astype(BF16), b.astype(BF16), preferred_element_type=F32)


def _mm_nt(a, b):
    return lax.dot_general(a.astype(BF16), b.astype(BF16), (((1,), (1,)), ((), ())),
                           preferred_element_type=F32)


def _mm_tn(a, b):
    return lax.dot_general(a.astype(BF16), b.astype(BF16), (((0,), (0,)), ((), ())),
                           preferred_element_type=F32)


def _sigmoid(x):
    return jax.nn.sigmoid(x)


def _silu(x):
    return x * _sigmoid(x)


def _softplus(x):
    return jnp.maximum(x, 0.0) + jnp.log(1.0 + jnp.exp(-jnp.abs(x)))


def _group_mean_sq(o, bd_bf16, width):
    x = o * o
    hi = x.astype(BF16)
    lo = (x - hi.astype(F32)).astype(BF16)
    r = jnp.dot(jnp.concatenate([hi, lo], axis=0), bd_bf16, preferred_element_type=F32)
    n = o.shape[0]
    return (r[:n] + r[n:]) * (1.0 / width)


def _rotary(x, cos_t, sin_t):
    lane = lax.broadcasted_iota(jnp.int32, (x.shape[0], LANES), 1)
    first = (lane % RET_HD) < (RET_HD // 2)
    parts = []
    for p in range(x.shape[1] // LANES):
        xp = x[:, LANES * p:LANES * (p + 1)]
        swapped = jnp.where(first, pltpu.roll(xp, LANES - RET_HD // 2, 1), pltpu.roll(xp, RET_HD // 2, 1))
        parts.append(xp * cos_t + swapped * sin_t)
    return jnp.concatenate(parts, axis=1)


def _chunk_cumsum(x, pos):
    d = 1
    while d < CHUNK:
        x = x + jnp.where(pos >= d, pltpu.roll(x, d, 0), 0.0)
        d *= 2
    return x


def _load_rows(ref, n):
    return jnp.concatenate([ref[pl.ds(j, n, stride=ROW_TILES), :] for j in range(ROW_TILES)], axis=1)


def _store_rows(ref, x):
    n = x.shape[0]
    for j in range(ROW_TILES):
        ref[pl.ds(j, n, stride=ROW_TILES), :] = x[:, LANES * j:LANES * (j + 1)]


def _combine(ya_ref, yb_ref, rt_ref):
    rt = rt_ref[...]
    n = rt.shape[0]
    return rt[:, TOP_K:TOP_K + 1] * _load_rows(ya_ref, n) + rt[:, TOP_K + 1:TOP_K + 2] * _load_rows(yb_ref, n)


def _inproj_kernel(*refs, n_add, tiles_per_seq):
    if n_add:
        (h_ref, ya_ref, yb_ref, rt_ref, g_ref, w_ref, cos_ref, sin_ref, convw_ref, alog_ref, dtb_ref, lb_ref,
         proj_ref, st_ref, hout_ref, cbuf) = refs
        x = h_ref[...] + _combine(ya_ref, yb_ref, rt_ref)
        hout_ref[...] = x
    else:
        (h_ref, g_ref, w_ref, cos_ref, sin_ref, convw_ref, alog_ref, dtb_ref, lb_ref,
         proj_ref, st_ref, cbuf) = refs
        x = h_ref[...]
    tm = x.shape[0]
    ms = jnp.mean(x * x, axis=-1, keepdims=True)
    xb = (x * lax.rsqrt(ms + EPS) * g_ref[...]).astype(BF16)

    def mm(lo, hi):
        return jnp.dot(xb, w_ref[:, lo:hi], preferred_element_type=F32)

    pos = lax.broadcasted_iota(jnp.int32, (tm, 1), 0) % CHUNK

    pr = mm(W_RET, W_RET + 4 * RET_W)
    cos_t = cos_ref[...]
    sin_t = sin_ref[...]
    q = pr[:, 0:RET_W]
    k = pr[:, RET_W:2 * RET_W]
    proj_ref[:, OFF_RET:OFF_RET + RET_W] = _rotary(q, cos_t, sin_t)
    proj_ref[:, OFF_RET + RET_W:OFF_RET + 2 * RET_W] = _rotary(k, cos_t, sin_t) * (RET_HD ** -0.5)
    proj_ref[:, OFF_RET + 2 * RET_W:OFF_RET + 3 * RET_W] = pr[:, 2 * RET_W:3 * RET_W]
    proj_ref[:, OFF_RET + 3 * RET_W:OFF_RET + 4 * RET_W] = _silu(pr[:, 3 * RET_W:4 * RET_W])

    first = (pl.program_id(0) % tiles_per_seq) == 0

    @pl.when(first)
    def _():
        cbuf[0:SUBLANES, :] = jnp.zeros((SUBLANES, 3 * GDN_W), F32)

    @pl.when(jnp.logical_not(first))
    def _():
        cbuf[0:SUBLANES, :] = cbuf[tm:tm + SUBLANES, :]

    cbuf[SUBLANES:SUBLANES + tm, :] = mm(W_GDN, W_GDN + 3 * GDN_W)
    y = cbuf[SUBLANES:SUBLANES + tm, :] * convw_ref[CONV_W - 1:CONV_W, :]
    for j in range(CONV_W - 1):
        lo = SUBLANES - (CONV_W - 1) + j
        y = y + cbuf[lo:lo + tm, :] * convw_ref[j:j + 1, :]
    y = _silu(y)
    for h in range(GDN_HEADS):
        cq = y[:, GDN_HD * h:GDN_HD * (h + 1)]
        ck = y[:, GDN_W + GDN_HD * h:GDN_W + GDN_HD * (h + 1)]
        proj_ref[:, OFF_GDN + GDN_HD * h:OFF_GDN + GDN_HD * (h + 1)] = (
            cq * (lax.rsqrt(jnp.sum(cq * cq, axis=-1, keepdims=True) + EPS) * (GDN_HD ** -0.5)))
        proj_ref[:, OFF_GDN + GDN_W + GDN_HD * h:OFF_GDN + GDN_W + GDN_HD * (h + 1)] = (
            ck * lax.rsqrt(jnp.sum(ck * ck, axis=-1, keepdims=True) + EPS))
    proj_ref[:, OFF_GDN + 2 * GDN_W:OFF_GDN + 3 * GDN_W] = y[:, 2 * GDN_W:3 * GDN_W]
    proj_ref[:, OFF_GDN + 3 * GDN_W:OFF_GDN + 4 * GDN_W] = _silu(mm(W_GDN + 3 * GDN_W, W_GDN + 4 * GDN_W))

    ph = mm(W_HG, W_HG + 4 * HG_W)
    lb = lb_ref[...]
    f = lb + (1.0 - lb) * _sigmoid(ph[:, HG_W:2 * HG_W])
    proj_ref[:, OFF_HG:OFF_HG + HG_W] = _silu(ph[:, 0:HG_W])
    proj_ref[:, OFF_HG + HG_W:OFF_HG + 2 * HG_W] = 1.0 - f
    proj_ref[:, OFF_HG + 2 * HG_W:OFF_HG + 3 * HG_W] = ph[:, 2 * HG_W:3 * HG_W]
    proj_ref[:, OFF_HG + 3 * HG_W:OFF_HG + 4 * HG_W] = _silu(ph[:, 3 * HG_W:4 * HG_W])
    proj_ref[:, OFF_HG + 4 * HG_W:OFF_HG + 5 * HG_W] = _chunk_cumsum(jnp.log(f), pos)

    small = mm(W_SMALL, W_SMALL + LANES)
    gcum = _chunk_cumsum(-jnp.exp(alog_ref[...]) * _softplus(small + dtb_ref[...]), pos)
    lane = lax.broadcasted_iota(jnp.int32, (tm, LANES), 1)
    proj_ref[:, OFF_SMALL:OFF_SMALL + LANES] = jnp.where(lane < GDN_HEADS, gcum, _sigmoid(small))
    gt = gcum.T
    for c in range(tm // CHUNK):
        st_ref[c] = gt[0:SUBLANES, CHUNK * c:CHUNK * (c + 1)]


def _inproj(h, y2, rout, gain, w_bf16, consts, conv_w, a_log, dt_bias, lb, seq, tm):
    t = h.shape[0]
    n_add = y2 is not None
    tps = seq // tm
    cos_t, sin_t = consts[0], consts[1]
    pad4 = lambda a: jnp.concatenate([a.astype(F32), jnp.zeros((LANES - GDN_HEADS,), F32)]).reshape(1, LANES)
    const = lambda shape: pl.BlockSpec(shape, lambda i: (0,) * len(shape))
    row = pl.BlockSpec((tm, D_MODEL), lambda i: (i, 0))
    in_specs = [row]
    args = [h]
    if n_add:
        in_specs += [pl.BlockSpec((tm * ROW_TILES, LANES), lambda i: (i, 0)),
                     pl.BlockSpec((tm * ROW_TILES, LANES), lambda i: (t // tm + i, 0)),
                     pl.BlockSpec((tm, LANES), lambda i: (i, 0))]
        args += [y2, y2, rout]
    in_specs += [const((1, D_MODEL)),
                 pl.BlockSpec((D_MODEL, W_COLS), lambda i: (0, 0), pipeline_mode=pl.Buffered(1)),
                 pl.BlockSpec((tm, LANES), lambda i: (i % tps, 0)),
                 pl.BlockSpec((tm, LANES), lambda i: (i % tps, 0)),
                 const((CONV_W, 3 * GDN_W)), const((1, LANES)), const((1, LANES)), const((1, HG_W))]
    args += [gain.reshape(1, D_MODEL), w_bf16, cos_t, sin_t, conv_w.astype(F32), pad4(a_log), pad4(dt_bias),
             lb.reshape(1, HG_W).astype(F32)]
    out_shape = [jax.ShapeDtypeStruct((t, PROJ_W), F32),
                 jax.ShapeDtypeStruct((t // CHUNK, SUBLANES, CHUNK), F32)]
    out_specs = [pl.BlockSpec((tm, PROJ_W), lambda i: (i, 0)),
                 pl.BlockSpec((tm // CHUNK, SUBLANES, CHUNK), lambda i: (i, 0, 0))]
    if n_add:
        out_shape.append(jax.ShapeDtypeStruct((t, D_MODEL), F32))
        out_specs.append(row)
    res = pl.pallas_call(
        functools.partial(_inproj_kernel, n_add=n_add, tiles_per_seq=tps),
        grid=(t // tm,),
        in_specs=in_specs,
        out_specs=out_specs,
        out_shape=out_shape,
        scratch_shapes=[pltpu.VMEM((tm + SUBLANES, 3 * GDN_W), F32)],
        compiler_params=pltpu.CompilerParams(dimension_semantics=("arbitrary",),
                                             vmem_limit_bytes=VMEM_LIMIT),
        name="inproj",
    )(*args)
    return res


def _mixer_kernel(proj_ref, st_ref, dmat_ref, xi_ref, zeta_ref, rdec_ref, bd_ref, bdb_ref,
                  retg_ref, gdng_ref, hgg_ref, out_ref, ret_s, gdn_s, hg_s):
    @pl.when(pl.program_id(0) == 0)
    def _():
        ret_s[...] = jnp.zeros_like(ret_s)
        gdn_s[...] = jnp.zeros_like(gdn_s)
        hg_s[...] = jnp.zeros_like(hg_s)

    nsub = proj_ref.shape[1] // CHUNK
    gens = []
    for j in range(nsub):
        rows = pl.ds(CHUNK * j, CHUNK)
        for b in range(proj_ref.shape[0]):
            pr = proj_ref.at[b, rows]
            ou = out_ref.at[b, rows]
            gens.append((2 * j, _gdn_chunk(pr, st_ref.at[b, j], ou, gdn_s.at[b], gdng_ref)))
            gens.append((2 * j, _hgrn_chunk(pr, ou, hg_s.at[b], bd_ref, bdb_ref, hgg_ref)))
            gens.append((2 * j, _ret_chunk(pr, ou, ret_s.at[b], dmat_ref, xi_ref, zeta_ref, rdec_ref,
                                           bd_ref, bdb_ref, retg_ref)))
    rnd = 0
    while gens:
        alive = []
        for d, g in gens:
            if rnd >= d:
                try:
                    next(g)
                except StopIteration:
                    continue
            alive.append((d, g))
        gens = alive
        rnd += 1


def _ret_chunk(proj_ref, out_ref, ret_s, dmat_ref, xi_ref, zeta_ref, rdec_ref, bd_ref, bdb_ref, retg_ref):
    c = CHUNK
    head4 = lax.broadcasted_iota(jnp.int32, (c, RET_W), 1) // RET_HD
    qr = proj_ref[:, OFF_RET:OFF_RET + RET_W]
    kr = proj_ref[:, OFF_RET + RET_W:OFF_RET + 2 * RET_W]
    v = proj_ref[:, OFF_RET + 2 * RET_W:OFF_RET + 3 * RET_W]
    s_ret = ret_s[...]
    o = _mm(qr * xi_ref[...], s_ret)
    scs = [_mm_nt(jnp.where(head4 == h, qr, 0.0), kr) * dmat_ref[h] for h in range(RET_HEADS)]
    yield
    ohs = [_mm(sc, v) for sc in scs]
    ret_s[...] = s_ret * rdec_ref[...] + bd_ref[...] * _mm_tn(kr * zeta_ref[...], v)
    yield
    for h in range(RET_HEADS):
        o = o + jnp.where(head4 == h, ohs[h], 0.0)
    ms = _group_mean_sq(o, bdb_ref[...], RET_HD)
    yield
    out_ref[:, 0:RET_W] = (o * lax.rsqrt(ms + EPS) * retg_ref[...]
                           * proj_ref[:, OFF_RET + 3 * RET_W:OFF_RET + 4 * RET_W])


def _gdn_chunk(proj_ref, st_ref, out_ref, gdn_s, gdng_ref):
    c = CHUNK
    nh = GDN_HEADS
    hs = range(nh)
    r64 = lax.broadcasted_iota(jnp.int32, (c, c), 0)
    c64 = lax.broadcasted_iota(jnp.int32, (c, c), 1)
    causal = r64 >= c64
    strict = r64 > c64
    eye = (r64 == c64).astype(F32)
    qn = [proj_ref[:, OFF_GDN + GDN_HD * h:OFF_GDN + GDN_HD * (h + 1)] for h in hs]
    kn = [proj_ref[:, OFF_GDN + GDN_W + GDN_HD * h:OFF_GDN + GDN_W + GDN_HD * (h + 1)] for h in hs]
    small = proj_ref[:, OFF_SMALL:OFF_SMALL + LANES]
    gcol = [small[:, h:h + 1] for h in hs]
    bcol = [small[:, nh + h:nh + h + 1] for h in hs]
    gamma = [jnp.where(causal, jnp.exp(jnp.minimum(gcol[h] - st_ref[h:h + 1, :], 0.0)), 0.0) for h in hs]
    kk = [_mm_nt(kn[h], kn[h]) for h in hs]
    qk = [_mm_nt(qn[h], kn[h]) * gamma[h] for h in hs]
    yield
    p = [jnp.where(strict, bcol[h] * kk[h] * gamma[h], 0.0) for h in hs]
    t_inv = [eye - p[h] for h in hs]
    eg = [jnp.exp(gcol[h]) for h in hs]
    cv = [proj_ref[:, OFF_GDN + 2 * GDN_W + GDN_HD * h:OFF_GDN + 2 * GDN_W + GDN_HD * (h + 1)] for h in hs]
    rhs = [jnp.concatenate([cv[h] * bcol[h], kn[h] * (bcol[h] * eg[h])], axis=1) for h in hs]
    glast = [small[c - 1:c, h:h + 1] for h in hs]
    kd = [kn[h] * jnp.exp(glast[h] - gcol[h]) for h in hs]
    qd = [qn[h] * eg[h] for h in hs]
    for _ in range(5):
        p = [_mm(x, x) for x in p]
        yield
        t_inv = [t_inv[h] + _mm(t_inv[h], p[h]) for h in hs]
        yield
    sol = [_mm(t_inv[h], rhs[h]) for h in hs]
    yield
    s_g = [gdn_s[h] for h in hs]
    v_new = [sol[h][:, :GDN_HD] - _mm(sol[h][:, GDN_HD:], s_g[h]) for h in hs]
    oh = [_mm(qd[h], s_g[h]) for h in hs]
    yield
    oh = [oh[h] + _mm(qk[h], v_new[h]) for h in hs]
    for h in hs:
        gdn_s[h] = s_g[h] * jnp.exp(glast[h]) + _mm_tn(kd[h], v_new[h])
    yield
    for h in hs:
        msd = jnp.mean(oh[h] * oh[h], axis=-1, keepdims=True)
        dzs = proj_ref[:, OFF_GDN + 3 * GDN_W + GDN_HD * h:OFF_GDN + 3 * GDN_W + GDN_HD * (h + 1)]
        out_ref[:, RET_W + GDN_HD * h:RET_W + GDN_HD * (h + 1)] = oh[h] * lax.rsqrt(msd + EPS) * gdng_ref[...] * dzs


def _hgrn_chunk(proj_ref, out_ref, hg_s, bd_ref, bdb_ref, hgg_ref):
    c = CHUNK
    head4 = lax.broadcasted_iota(jnp.int32, (c, HG_W), 1) // HG_HD
    qh = proj_ref[:, OFF_HG:OFF_HG + HG_W]
    kh = proj_ref[:, OFF_HG + HG_W:OFF_HG + 2 * HG_W]
    hv = proj_ref[:, OFF_HG + 2 * HG_W:OFF_HG + 3 * HG_W]
    gl = proj_ref[:, OFF_HG + 4 * HG_W:OFF_HG + 5 * HG_W]
    row = lax.broadcasted_iota(jnp.int32, (c, HG_W), 0)
    blk = row // SUB_BLOCK
    rin = row % SUB_BLOCK
    nblk = c // SUB_BLOCK
    s_h = hg_s[...]
    o = _mm_nt(qh * jnp.exp(gl), s_h)
    glast = gl[c - 1:c, :]
    hg_s[...] = s_h * jnp.exp(glast) + bd_ref[...] * _mm_tn(hv, kh * jnp.exp(glast - gl))
    yield
    gref = gl[SUB_BLOCK - 1:SUB_BLOCK, :]
    for i in range(2, nblk):
        gref = jnp.where(blk >= i, gl[SUB_BLOCK * i - 1:SUB_BLOCK * i, :], gref)
    qs = qh * jnp.exp(jnp.minimum(gl - gref, 0.0))
    ks = []
    for i in range(1, nblk):
        gi = gl[SUB_BLOCK * i - 1:SUB_BLOCK * i, :]
        ks.append(jnp.where(row < SUB_BLOCK * i, kh * jnp.exp(jnp.minimum(gi - gl, 0.0)), 0.0))
    kcat = jnp.concatenate(ks, axis=1)
    yield
    atts = []
    for h in range(HG_HEADS):
        mh = head4 == h
        qcat = jnp.concatenate([jnp.where(mh & (blk == i), qs, 0.0) for i in range(1, nblk)], axis=1)
        atts.append(_mm_nt(qcat, kcat))
    yield
    offs = [_mm(a, hv) for a in atts]
    yield
    for h in range(HG_HEADS):
        o = o + jnp.where(head4 == h, offs[h], 0.0)
    ps = [qh * kh]
    for j in range(1, SUB_BLOCK):
        e = jnp.exp(jnp.minimum(gl - pltpu.roll(gl, j, 0), 0.0))
        ps.append(jnp.where(rin >= j, qh * e * pltpu.roll(kh, j, 0), 0.0))
        if j % 4 == 0:
            yield
    acat = jnp.dot(jnp.concatenate(ps, axis=0).astype(BF16), bdb_ref[...], preferred_element_type=F32)
    yield
    o = o + acat[0:c] * hv
    for j in range(1, SUB_BLOCK):
        o = o + acat[c * j:c * (j + 1)] * pltpu.roll(hv, j, 0)
        if j % 4 == 0:
            yield
    ms = _group_mean_sq(o, bdb_ref[...], HG_HD)
    yield
    out_ref[:, RET_W + GDN_W:MIX_W] = (o * lax.rsqrt(ms + EPS) * hgg_ref[...]
                                      * proj_ref[:, OFF_HG + 3 * HG_W:OFF_HG + 4 * HG_W])


def _mixer_consts(seq):
    with jax.ensure_compile_time_eval():
        return _mixer_consts_eval(seq)


def _mixer_consts_eval(seq):
    hd = RET_HD
    inv = ROPE_BASE ** (-jnp.arange(0, hd, 2, dtype=F32) / hd)
    ang = jnp.arange(seq, dtype=F32)[:, None] * inv[None, :]
    cos = jnp.cos(ang)
    sin = jnp.sin(ang)
    cos_t = jnp.tile(jnp.concatenate([cos, cos], axis=1), (1, LANES // RET_HD))
    sin_t = jnp.tile(jnp.concatenate([-sin, sin], axis=1), (1, LANES // RET_HD))
    log_gamma = jnp.log(1.0 - 2.0 ** (-5.0 - jnp.arange(RET_HEADS, dtype=F32)))
    idx = jnp.arange(CHUNK, dtype=F32)
    diff = idx[:, None] - idx[None, :]
    cm = diff >= 0
    dmat = jnp.where(cm, jnp.exp(log_gamma[:, None, None] * jnp.where(cm, diff, 0.0)), 0.0)
    zeta = jnp.exp(log_gamma[:, None] * (CHUNK - 1.0 - idx))
    xi = jnp.exp(log_gamma[:, None] * (idx + 1.0))
    zeta_t = jnp.repeat(zeta.T, RET_HD, axis=1)
    xi_t = jnp.repeat(xi.T, RET_HD, axis=1)
    rdec = jnp.repeat(jnp.exp(log_gamma * CHUNK), RET_HD)[None, :]
    hid = np.arange(RET_W) // RET_HD
    bd = jnp.asarray((hid[:, None] == hid[None, :]).astype(np.float32))
    return cos_t, sin_t, dmat, xi_t, zeta_t, rdec, bd, bd.astype(BF16)


def _mixer(proj, st, consts, ret_g, gdn_g, hg_g, batch, seq):
    _, _, dmat, xi_t, zeta_t, rdec, bd, bd_bf16 = consts
    n = seq // CHUNK
    const = lambda shape: pl.BlockSpec(shape, lambda i: (0,) * len(shape))
    in_specs = [
        pl.BlockSpec((batch, MIX_TILE, PROJ_W), lambda i: (0, i, 0)),
        pl.BlockSpec((batch, MIX_TILE // CHUNK, SUBLANES, CHUNK), lambda i: (0, i, 0, 0)),
        const((RET_HEADS, CHUNK, CHUNK)),
        const((CHUNK, RET_W)),
        const((CHUNK, RET_W)),
        const((1, RET_W)),
        const((RET_W, RET_W)),
        const((RET_W, RET_W)),
        const((1, RET_W)),
        const((1, GDN_HD)),
        const((1, HG_W)),
    ]
    return pl.pallas_call(
        _mixer_kernel,
        grid=(seq // MIX_TILE,),
        in_specs=in_specs,
        out_specs=pl.BlockSpec((batch, MIX_TILE, MIX_W), lambda i: (0, i, 0)),
        out_shape=jax.ShapeDtypeStruct((batch, seq, MIX_W), F32),
        scratch_shapes=[
            pltpu.VMEM((batch, RET_W, RET_W), F32),
            pltpu.VMEM((batch, GDN_HEADS, GDN_HD, GDN_HD), F32),
            pltpu.VMEM((batch, HG_W, HG_W), F32),
        ],
        compiler_params=pltpu.CompilerParams(dimension_semantics=("arbitrary",),
                                             vmem_limit_bytes=VMEM_LIMIT),
        name="mixer",
    )(proj.reshape(batch, seq, PROJ_W), st.reshape(batch, n, SUBLANES, CHUNK), dmat, xi_t, zeta_t, rdec, bd,
      bd_bf16, ret_g.reshape(1, RET_W).astype(F32), gdn_g.reshape(1, GDN_HD).astype(F32),
      hg_g.reshape(1, HG_W).astype(F32))


def _post_kernel(mixed_ref, h_ref, wout_ref, g_ref, wr_ref, h1_ref, xn_ref, rout_ref):
    h1 = h_ref[...] + jnp.dot(mixed_ref[...].astype(BF16), wout_ref[...], preferred_element_type=F32)
    h1_ref[...] = h1
    ms = jnp.mean(h1 * h1, axis=-1, keepdims=True)
    xn = h1 * lax.rsqrt(ms + EPS) * g_ref[...]
    _store_rows(xn_ref, xn)
    logits = jnp.dot(xn.astype(BF16), wr_ref[...], preferred_element_type=F32)
    lt = logits.T
    tm = logits.shape[0]
    rowi = lax.broadcasted_iota(jnp.int32, (SUBLANES, tm), 0)
    big = jnp.int32(1 << 20)
    ninf = jnp.float32(-jnp.inf)
    gmask = rowi < N_GROUPS
    gl = jnp.where(gmask, lt[0:SUBLANES, :], ninf)
    eg = jnp.exp(gl - jnp.max(gl, axis=0, keepdims=True))
    pg = eg / jnp.sum(eg, axis=0, keepdims=True)
    pg_top = jnp.max(pg, axis=0, keepdims=True)
    grp = jnp.min(jnp.where(gmask & (pg == pg_top), rowi, big), axis=0, keepdims=True)
    el = lt[SUBLANES:2 * SUBLANES, :]
    for gi in range(1, N_GROUPS):
        el = jnp.where(grp == gi, lt[SUBLANES * (gi + 1):SUBLANES * (gi + 2), :], el)
    ee = jnp.exp(el - jnp.max(el, axis=0, keepdims=True))
    pe = ee / jnp.sum(ee, axis=0, keepdims=True)
    p1 = jnp.max(pe, axis=0, keepdims=True)
    i1 = jnp.min(jnp.where(pe == p1, rowi, big), axis=0, keepdims=True)
    pe2 = jnp.where(rowi == i1, -1.0, pe)
    p2 = jnp.max(pe2, axis=0, keepdims=True)
    i2 = jnp.min(jnp.where(pe2 == p2, rowi, big), axis=0, keepdims=True)
    den = p1 + p2
    w1 = pg_top * (p1 / den)
    w2 = pg_top * (p2 / den)
    e1 = (grp * EXP_PER_GROUP + i1).astype(F32)
    e2 = (grp * EXP_PER_GROUP + i2).astype(F32)
    res = jnp.where(rowi == 0, e1, jnp.where(rowi == 1, e2, jnp.where(rowi == 2, w1, jnp.where(rowi == 3, w2, 0.0))))
    rout_ref[...] = jnp.concatenate([res, jnp.zeros((LANES - SUBLANES, tm), F32)], axis=0).T


def _post(mixed, h, wout_bf16, gain, wr_bf16, tm):
    t = h.shape[0]
    row = pl.BlockSpec((tm, D_MODEL), lambda i: (i, 0))
    return pl.pallas_call(
        _post_kernel,
        grid=(t // tm,),
        in_specs=[row, row,
                  pl.BlockSpec((MIX_W, D_MODEL), lambda i: (0, 0)),
                  pl.BlockSpec((1, D_MODEL), lambda i: (0, 0)),
                  pl.BlockSpec((D_MODEL, LANES), lambda i: (0, 0))],
        out_specs=[row, pl.BlockSpec((tm * ROW_TILES, LANES), lambda i: (i, 0)),
                   pl.BlockSpec((tm, LANES), lambda i: (i, 0))],
        out_shape=[jax.ShapeDtypeStruct((t, D_MODEL), F32), jax.ShapeDtypeStruct((t * ROW_TILES, LANES), F32),
                   jax.ShapeDtypeStruct((t, LANES), F32)],
        compiler_params=pltpu.CompilerParams(dimension_semantics=("arbitrary",),
                                             vmem_limit_bytes=VMEM_LIMIT),
        name="post",
    )(mixed, h, wout_bf16, gain.reshape(1, D_MODEL), wr_bf16)


def _moe_kernel(bexp_ref, tok_ref, tokn_ref, dstp_ref, dst_ref, xn_hbm,
                wg_ref, wu_ref, wd_ref, y_hbm, xbuf, ybuf, wgb, wub, wdb, gsem, ssem, *, dummy_row):
    i = pl.program_id(0)
    nb = pl.num_programs(0)
    blk = MOE_BLOCK
    row_c = lax.rem(i, SUBLANES)
    row_n = lax.rem(jnp.minimum(i + 1, nb - 1), SUBLANES)
    row_p = lax.rem(jnp.maximum(i - 1, 0), SUBLANES)
    n_piece = D_EXPERT // LANES
    per = blk // n_piece

    def gather_copy(t, r, s):
        return pltpu.make_async_copy(xn_hbm.at[pl.ds(pl.multiple_of(t, ROW_TILES), ROW_TILES)],
                                     xbuf.at[s, pl.ds(ROW_TILES * r, ROW_TILES)], gsem.at[s])

    def scatter_copy(d, r, s):
        return pltpu.make_async_copy(ybuf.at[s, pl.ds(ROW_TILES * r, ROW_TILES)],
                                     y_hbm.at[pl.ds(pl.multiple_of(d, ROW_TILES), ROW_TILES)], ssem.at[s])

    @pl.when(i == 0)
    def _():
        ybuf[1] = jnp.zeros((blk * ROW_TILES, LANES), F32)
        for r in range(blk):
            gather_copy(tok_ref[row_c, r], r, 0).start(priority=r % 2)

    @pl.when((i == 0) | (bexp_ref[i] != bexp_ref[jnp.maximum(i - 1, 0)]))
    def _():
        wgb[...] = wg_ref[...].astype(BF16)
        wub[...] = wu_ref[...].astype(BF16)
        wdb[...] = wd_ref[...].astype(BF16)

    def step(s):
        for r in range(blk):
            gather_copy(0, r, s).wait()
        first = i == 0
        x = _load_rows(xbuf.at[s], blk).astype(BF16)
        parts = []
        for c in range(n_piece):
            cs = slice(LANES * c, LANES * (c + 1))
            hg = jnp.dot(x, wgb[:, cs], preferred_element_type=F32)
            hu = jnp.dot(x, wub[:, cs], preferred_element_type=F32)
            parts.append((_silu(hg) * hu).astype(BF16))
            for r in range(per * c, per * (c + 1)):
                gather_copy(tokn_ref[row_n, r], r, 1 - s).start(priority=r % 2)
                scatter_copy(jnp.where(first, dummy_row + ROW_TILES * r, dstp_ref[row_p, r]), r, 1 - s).start(priority=r % 2)
        hb = jnp.concatenate(parts, axis=1)
        _store_rows(ybuf.at[s], jnp.dot(hb, wdb[...], preferred_element_type=F32))
        for r in range(blk):
            scatter_copy(0, r, 1 - s).wait()

        @pl.when(i == nb - 1)
        def _():
            for r in range(blk):
                scatter_copy(dst_ref[row_c, r], r, s).start(priority=r % 2)
            for r in range(blk):
                scatter_copy(0, r, s).wait()
            for r in range(blk):
                gather_copy(0, r, 1 - s).wait()

    @pl.when(lax.rem(i, 2) == 0)
    def _():
        step(0)

    @pl.when(lax.rem(i, 2) == 1)
    def _():
        step(1)


def _experts(xn, tok, dst, bexp, w_gate, w_up, w_down, layer):
    t = xn.shape[0] // ROW_TILES
    nb = bexp.shape[0]
    blk = MOE_BLOCK
    smem_blk = lambda f: pl.BlockSpec((SUBLANES, blk), f, memory_space=pltpu.SMEM)
    grid_spec = pltpu.PrefetchScalarGridSpec(
        num_scalar_prefetch=1,
        grid=(nb,),
        in_specs=[
            smem_blk(lambda i, be: (i // SUBLANES, 0)),
            smem_blk(lambda i, be: (jnp.minimum(i + 1, nb - 1) // SUBLANES, 0)),
            smem_blk(lambda i, be: (jnp.maximum(i - 1, 0) // SUBLANES, 0)),
            smem_blk(lambda i, be: (i // SUBLANES, 0)),
            pl.BlockSpec(memory_space=pl.ANY),
            pl.BlockSpec((None, None, D_MODEL, D_EXPERT), lambda i, be: (layer, be[i], 0, 0)),
            pl.BlockSpec((None, None, D_MODEL, D_EXPERT), lambda i, be: (layer, be[i], 0, 0)),
            pl.BlockSpec((None, None, D_EXPERT, D_MODEL), lambda i, be: (layer, be[i], 0, 0)),
        ],
        out_specs=pl.BlockSpec(memory_space=pl.ANY),
        scratch_shapes=[
            pltpu.VMEM((2, blk * ROW_TILES, LANES), F32),
            pltpu.VMEM((2, blk * ROW_TILES, LANES), F32),
            pltpu.VMEM((D_MODEL, D_EXPERT), BF16),
            pltpu.VMEM((D_MODEL, D_EXPERT), BF16),
            pltpu.VMEM((D_EXPERT, D_MODEL), BF16),
            pltpu.SemaphoreType.DMA((2,)),
            pltpu.SemaphoreType.DMA((2,)),
        ],
    )
    tok2 = tok.reshape(nb, blk)
    dst2 = dst.reshape(nb, blk)
    return pl.pallas_call(
        functools.partial(_moe_kernel, dummy_row=TOP_K * t * ROW_TILES),
        grid_spec=grid_spec,
        out_shape=jax.ShapeDtypeStruct(((TOP_K * t + blk) * ROW_TILES, LANES), F32),
        compiler_params=pltpu.CompilerParams(dimension_semantics=("arbitrary",),
                                             vmem_limit_bytes=VMEM_LIMIT),
        name="experts",
    )(bexp, tok2, tok2, dst2, dst2, xn, w_gate, w_up, w_down)


def _dispatch(rout, t):
    n_assign = t * TOP_K
    e_flat = rout[:, 0:TOP_K].astype(jnp.int32).reshape(n_assign)
    e_sorted, order = lax.sort_key_val(e_flat, jnp.arange(n_assign, dtype=jnp.int32))
    start_raw = jnp.searchsorted(e_sorted, jnp.arange(N_EXPERTS + 1, dtype=jnp.int32), side='left',
                                 method='compare_all').astype(jnp.int32)
    counts = start_raw[1:] - start_raw[:-1]
    padded = (counts + MOE_BLOCK - 1) // MOE_BLOCK * MOE_BLOCK
    ends_pad = jnp.cumsum(padded)
    start_pad = ends_pad - padded
    n_blocks = (n_assign + N_EXPERTS * (MOE_BLOCK - 1) + MOE_BLOCK - 1) // MOE_BLOCK
    n_blocks = (n_blocks + SUBLANES - 1) // SUBLANES * SUBLANES
    blk_start = jnp.arange(n_blocks, dtype=jnp.int32) * MOE_BLOCK
    bexp = jnp.minimum(jnp.searchsorted(ends_pad, blk_start, side='right', method='compare_all'),
                       N_EXPERTS - 1).astype(jnp.int32)
    slot = jnp.arange(n_blocks * MOE_BLOCK, dtype=jnp.int32)
    se = jnp.repeat(bexp, MOE_BLOCK)
    rank = slot - start_pad[se]
    valid = (rank >= 0) & (rank < counts[se])
    a = order[jnp.clip(start_raw[se] + rank, 0, n_assign - 1)]
    tok = (jnp.where(valid, a // TOP_K, 0) * ROW_TILES).astype(jnp.int32)
    dst = (jnp.where(valid, (a % TOP_K) * t + a // TOP_K, TOP_K * t + slot % MOE_BLOCK) * ROW_TILES).astype(jnp.int32)
    return tok, dst, bexp


def _final_kernel(h_ref, ya_ref, yb_ref, rt_ref, g_ref, o_ref):
    x = h_ref[...] + _combine(ya_ref, yb_ref, rt_ref)
    ms = jnp.mean(x * x, axis=-1, keepdims=True)
    o_ref[...] = x * lax.rsqrt(ms + EPS) * g_ref[...]


def _final(h1, y2, rout, gain, tm):
    t = h1.shape[0]
    row = pl.BlockSpec((tm, D_MODEL), lambda i: (i, 0))
    return pl.pallas_call(
        _final_kernel,
        grid=(t // tm,),
        in_specs=[row,
                  pl.BlockSpec((tm * ROW_TILES, LANES), lambda i: (i, 0)),
                  pl.BlockSpec((tm * ROW_TILES, LANES), lambda i: (t // tm + i, 0)),
                  pl.BlockSpec((tm, LANES), lambda i: (i, 0)),
                  pl.BlockSpec((1, D_MODEL), lambda i: (0, 0))],
        out_specs=row,
        out_shape=jax.ShapeDtypeStruct((t, D_MODEL), F32),
        compiler_params=pltpu.CompilerParams(dimension_semantics=("arbitrary",)),
        name="final_norm",
    )(h1, y2, y2, rout, gain.reshape(1, D_MODEL))


def _prep_w_kernel(w_ref, o_ref):
    a = 4 * RET_W + 3 * GDN_W
    n_small = 2 * GDN_HEADS
    rest = w_ref.shape[1] - a - n_small
    o_ref[:, 0:a] = w_ref[:, 0:a].astype(BF16)
    o_ref[:, a:a + rest] = w_ref[:, a + n_small:a + n_small + rest].astype(BF16)
    lane = lax.broadcasted_iota(jnp.int32, (w_ref.shape[0], LANES), 1)
    o_ref[:, W_SMALL:W_SMALL + LANES] = jnp.where(lane < n_small, w_ref[:, a:a + LANES], 0.0).astype(BF16)


def _prep_w_in(w, layer):
    rows = 128
    return pl.pallas_call(
        _prep_w_kernel,
        grid=(w.shape[1] // rows,),
        in_specs=[pl.BlockSpec((None, rows, w.shape[2]), lambda i: (layer, i, 0))],
        out_specs=pl.BlockSpec((rows, W_COLS), lambda i: (i, 0)),
        out_shape=jax.ShapeDtypeStruct((w.shape[1], W_COLS), BF16),
        name="prep_w_in",
    )(w)


def kernel(x, w_in, w_out, gdn_conv, gdn_a_log, gdn_dt_bias, ret_norm, gdn_norm, hgrn_norm,
           hgrn_lower_bounds, attn_norm, ffn_norm, router_group, router_expert,
           w_gate, w_up, w_down, final_norm):
    batch, seq, dm = x.shape
    t = batch * seq
    tm = min(512, seq)
    lb_all = jnp.cumsum(jax.nn.softmax(hgrn_lower_bounds.astype(F32), axis=0), axis=0)
    lb_all = lb_all - lb_all[0:1]
    consts = _mixer_consts(seq)
    h = x.reshape(t, dm)
    h1 = None
    y2 = None
    rout = None
    for l in range(DEPTH):
        res = _inproj(h if l == 0 else h1, y2, rout, attn_norm[l], _prep_w_in(w_in, l), consts, gdn_conv[l],
                      gdn_a_log[l], gdn_dt_bias[l], lb_all[l], seq, tm)
        if l == 0:
            proj, st = res
        else:
            proj, st, h = res
        mixed = _mixer(proj, st, consts, ret_norm[l], gdn_norm[l], hgrn_norm[l], batch, seq).reshape(t, MIX_W)
        wr = jnp.concatenate([router_group[l], jnp.zeros((dm, SUBLANES - N_GROUPS), F32), router_expert[l],
                              jnp.zeros((dm, LANES - SUBLANES - N_EXPERTS), F32)], axis=1).astype(BF16)
        h1, xn, rout = _post(mixed, h, w_out[l].astype(BF16), ffn_norm[l], wr, tm)
        tok, dst, bexp = _dispatch(rout, t)
        y2 = _experts(xn, tok, dst, bexp, w_gate, w_up, w_down, l)
    out = _final(h1, y2, rout, final_norm, tm)
    return out.reshape(batch, seq, dm)
```

```python
import functools

import jax
import jax.numpy as jnp
import numpy as np
from jax import lax
from jax.experimental import pallas as pl
from jax.experimental.pallas import tpu as pltpu

D_MODEL = 1024
DEPTH = 2
RET_HEADS, RET_HD = 4, 64
GDN_HEADS, GDN_HD = 4, 128
HG_HEADS, HG_HD = 4, 64
RET_W = RET_HEADS * RET_HD
GDN_W = GDN_HEADS * GDN_HD
HG_W = HG_HEADS * HG_HD
MIX_W = RET_W + GDN_W + HG_W
CHUNK = 64
CONV_W = 4
ROPE_BASE = 10000.0
N_GROUPS = 4
EXP_PER_GROUP = 8
N_EXPERTS = N_GROUPS * EXP_PER_GROUP
TOP_K = 2
D_EXPERT = 512
MOE_BLOCK = 256
MOE_BLOCK_LOG2 = MOE_BLOCK.bit_length() - 1
assert TOP_K == 2 and (1 << MOE_BLOCK_LOG2) == MOE_BLOCK
EPS = 1e-6

LANES = 128
SUBLANES = 8
ROW_TILES = D_MODEL // LANES
MIX_TILE = 256
SUB_BLOCK = 16

W_RET = 0
W_GDN = 4 * RET_W
W_HG = W_GDN + 4 * GDN_W
W_SMALL = W_HG + 4 * HG_W
W_COLS = W_SMALL + LANES

OFF_RET = 0
OFF_GDN = 4 * RET_W
OFF_HG = OFF_GDN + 4 * GDN_W
OFF_SMALL = OFF_HG + 5 * HG_W
PROJ_W = OFF_SMALL + LANES

VMEM_LIMIT = 56 * 1024 * 1024

F32 = jnp.float32
BF16 = jnp.bfloat16


def _mm(a, b):
    return jnp.dot(a.astype(BF16), b.astype(BF16), preferred_element_type=F32)


def _mm_nt(a, b):
    return lax.dot_general(a.astype(BF16), b.astype(BF16), (((1,), (1,)), ((), ())),
                           preferred_element_type=F32)


def _mm_tn(a, b):
    return lax.dot_general(a.astype(BF16), b.astype(BF16), (((0,), (0,)), ((), ())),
                           preferred_element_type=F32)


def _sigmoid(x):
    return jax.nn.sigmoid(x)


def _silu(x):
    return x * _sigmoid(x)


def _softplus(x):
    return jnp.maximum(x, 0.0) + jnp.log(1.0 + jnp.exp(-jnp.abs(x)))


def _group_mean_sq(o, bd_bf16, width):
    x = o * o
    hi = x.astype(BF16)
    lo = (x - hi.astype(F32)).astype(BF16)
    r = jnp.dot(jnp.concatenate([hi, lo], axis=0), bd_bf16, preferred_element_type=F32)
    n = o.shape[0]
    return (r[:n] + r[n:]) * (1.0 / width)


def _rotary(x, cos_t, sin_t):
    lane = lax.broadcasted_iota(jnp.int32, (x.shape[0], LANES), 1)
    first = (lane % RET_HD) < (RET_HD // 2)
    parts = []
    for p in range(x.shape[1] // LANES):
        xp = x[:, LANES * p:LANES * (p + 1)]
        swapped = jnp.where(first, pltpu.roll(xp, LANES - RET_HD // 2, 1), pltpu.roll(xp, RET_HD // 2, 1))
        parts.append(xp * cos_t + swapped * sin_t)
    return jnp.concatenate(parts, axis=1)


def _chunk_cumsum(x, pos):
    d = 1
    while d < CHUNK:
        x = x + jnp.where(pos >= d, pltpu.roll(x, d, 0), 0.0)
        d *= 2
    return x


def _load_rows(ref, n):
    return jnp.concatenate([ref[pl.ds(j, n, stride=ROW_TILES), :] for j in range(ROW_TILES)], axis=1)


def _store_rows(ref, x):
    n = x.shape[0]
    for j in range(ROW_TILES):
        ref[pl.ds(j, n, stride=ROW_TILES), :] = x[:, LANES * j:LANES * (j + 1)]


def _combine(ya_ref, yb_ref, rt_ref):
    rt = rt_ref[...]
    n = rt.shape[0]
    return rt[:, TOP_K:TOP_K + 1] * _load_rows(ya_ref, n) + rt[:, TOP_K + 1:TOP_K + 2] * _load_rows(yb_ref, n)


def _inproj_kernel(*refs, n_add, tiles_per_seq):
    if n_add:
        (h_ref, ya_ref, yb_ref, rt_ref, g_ref, w_ref, cos_ref, sin_ref, convw_ref, alog_ref, dtb_ref, lb_ref,
         proj_ref, st_ref, hout_ref, cbuf) = refs
        x = h_ref[...] + _combine(ya_ref, yb_ref, rt_ref)
        hout_ref[...] = x
    else:
        (h_ref, g_ref, w_ref, cos_ref, sin_ref, convw_ref, alog_ref, dtb_ref, lb_ref,
         proj_ref, st_ref, cbuf) = refs
        x = h_ref[...]
    tm = x.shape[0]
    ms = jnp.mean(x * x, axis=-1, keepdims=True)
    xb = (x * lax.rsqrt(ms + EPS) * g_ref[...]).astype(BF16)

    def mm(lo, hi):
        return jnp.dot(xb, w_ref[:, lo:hi], preferred_element_type=F32)

    pos = lax.broadcasted_iota(jnp.int32, (tm, 1), 0) % CHUNK

    pr = mm(W_RET, W_RET + 4 * RET_W)
    cos_t = cos_ref[...]
    sin_t = sin_ref[...]
    q = pr[:, 0:RET_W]
    k = pr[:, RET_W:2 * RET_W]
    proj_ref[:, OFF_RET:OFF_RET + RET_W] = _rotary(q, cos_t, sin_t)
    proj_ref[:, OFF_RET + RET_W:OFF_RET + 2 * RET_W] = _rotary(k, cos_t, sin_t) * (RET_HD ** -0.5)
    proj_ref[:, OFF_RET + 2 * RET_W:OFF_RET + 3 * RET_W] = pr[:, 2 * RET_W:3 * RET_W]
    proj_ref[:, OFF_RET + 3 * RET_W:OFF_RET + 4 * RET_W] = _silu(pr[:, 3 * RET_W:4 * RET_W])

    first = (pl.program_id(0) % tiles_per_seq) == 0

    @pl.when(first)
    def _():
        cbuf[0:SUBLANES, :] = jnp.zeros((SUBLANES, 3 * GDN_W), F32)

    @pl.when(jnp.logical_not(first))
    def _():
        cbuf[0:SUBLANES, :] = cbuf[tm:tm + SUBLANES, :]

    cbuf[SUBLANES:SUBLANES + tm, :] = mm(W_GDN, W_GDN + 3 * GDN_W)
    y = cbuf[SUBLANES:SUBLANES + tm, :] * convw_ref[CONV_W - 1:CONV_W, :]
    for j in range(CONV_W - 1):
        lo = SUBLANES - (CONV_W - 1) + j
        y = y + cbuf[lo:lo + tm, :] * convw_ref[j:j + 1, :]
    y = _silu(y)
    for h in range(GDN_HEADS):
        cq = y[:, GDN_HD * h:GDN_HD * (h + 1)]
        ck = y[:, GDN_W + GDN_HD * h:GDN_W + GDN_HD * (h + 1)]
        proj_ref[:, OFF_GDN + GDN_HD * h:OFF_GDN + GDN_HD * (h + 1)] = (
            cq * (lax.rsqrt(jnp.sum(cq * cq, axis=-1, keepdims=True) + EPS) * (GDN_HD ** -0.5)))
        proj_ref[:, OFF_GDN + GDN_W + GDN_HD * h:OFF_GDN + GDN_W + GDN_HD * (h + 1)] = (
            ck * lax.rsqrt(jnp.sum(ck * ck, axis=-1, keepdims=True) + EPS))
    proj_ref[:, OFF_GDN + 2 * GDN_W:OFF_GDN + 3 * GDN_W] = y[:, 2 * GDN_W:3 * GDN_W]
    proj_ref[:, OFF_GDN + 3 * GDN_W:OFF_GDN + 4 * GDN_W] = _silu(mm(W_GDN + 3 * GDN_W, W_GDN + 4 * GDN_W))

    ph = mm(W_HG, W_HG + 4 * HG_W)
    lb = lb_ref[...]
    f = lb + (1.0 - lb) * _sigmoid(ph[:, HG_W:2 * HG_W])
    proj_ref[:, OFF_HG:OFF_HG + HG_W] = _silu(ph[:, 0:HG_W])
    proj_ref[:, OFF_HG + HG_W:OFF_HG + 2 * HG_W] = 1.0 - f
    proj_ref[:, OFF_HG + 2 * HG_W:OFF_HG + 3 * HG_W] = ph[:, 2 * HG_W:3 * HG_W]
    proj_ref[:, OFF_HG + 3 * HG_W:OFF_HG + 4 * HG_W] = _silu(ph[:, 3 * HG_W:4 * HG_W])
    proj_ref[:, OFF_HG + 4 * HG_W:OFF_HG + 5 * HG_W] = _chunk_cumsum(jnp.log(f), pos)

    small = mm(W_SMALL, W_SMALL + LANES)
    gcum = _chunk_cumsum(-jnp.exp(alog_ref[...]) * _softplus(small + dtb_ref[...]), pos)
    lane = lax.broadcasted_iota(jnp.int32, (tm, LANES), 1)
    proj_ref[:, OFF_SMALL:OFF_SMALL + LANES] = jnp.where(lane < GDN_HEADS, gcum, _sigmoid(small))
    gt = gcum.T
    for c in range(tm // CHUNK):
        st_ref[c] = gt[0:SUBLANES, CHUNK * c:CHUNK * (c + 1)]


def _inproj(h, y2, rout, gain, w_bf16, consts, conv_w, a_log, dt_bias, lb, seq, tm):
    t = h.shape[0]
    n_add = y2 is not None
    tps = seq // tm
    cos_t, sin_t = consts[0], consts[1]
    pad4 = lambda a: jnp.concatenate([a.astype(F32), jnp.zeros((LANES - GDN_HEADS,), F32)]).reshape(1, LANES)
    const = lambda shape: pl.BlockSpec(shape, lambda i: (0,) * len(shape))
    row = pl.BlockSpec((tm, D_MODEL), lambda i: (i, 0))
    in_specs = [row]
    args = [h]
    if n_add:
        in_specs += [pl.BlockSpec((tm * ROW_TILES, LANES), lambda i: (i, 0)),
                     pl.BlockSpec((tm * ROW_TILES, LANES), lambda i: (t // tm + i, 0)),
                     pl.BlockSpec((tm, LANES), lambda i: (i, 0))]
        args += [y2, y2, rout]
    in_specs += [const((1, D_MODEL)),
                 pl.BlockSpec((D_MODEL, W_COLS), lambda i: (0, 0), pipeline_mode=pl.Buffered(1)),
                 pl.BlockSpec((tm, LANES), lambda i: (i % tps, 0)),
                 pl.BlockSpec((tm, LANES), lambda i: (i % tps, 0)),
                 const((CONV_W, 3 * GDN_W)), const((1, LANES)), const((1, LANES)), const((1, HG_W))]
    args += [gain.reshape(1, D_MODEL), w_bf16, cos_t, sin_t, conv_w.astype(F32), pad4(a_log), pad4(dt_bias),
             lb.reshape(1, HG_W).astype(F32)]
    out_shape = [jax.ShapeDtypeStruct((t, PROJ_W), F32),
                 jax.ShapeDtypeStruct((t // CHUNK, SUBLANES, CHUNK), F32)]
    out_specs = [pl.BlockSpec((tm, PROJ_W), lambda i: (i, 0)),
                 pl.BlockSpec((tm // CHUNK, SUBLANES, CHUNK), lambda i: (i, 0, 0))]
    if n_add:
        out_shape.append(jax.ShapeDtypeStruct((t, D_MODEL), F32))
        out_specs.append(row)
    res = pl.pallas_call(
        functools.partial(_inproj_kernel, n_add=n_add, tiles_per_seq=tps),
        grid=(t // tm,),
        in_specs=in_specs,
        out_specs=out_specs,
        out_shape=out_shape,
        scratch_shapes=[pltpu.VMEM((tm + SUBLANES, 3 * GDN_W), F32)],
        compiler_params=pltpu.CompilerParams(dimension_semantics=("arbitrary",),
                                             vmem_limit_bytes=VMEM_LIMIT),
        name="inproj",
    )(*args)
    return res


def _mixer_kernel(proj_ref, st_ref, dmat_ref, xi_ref, zeta_ref, rdec_ref, bd_ref, bdb_ref,
                  retg_ref, gdng_ref, hgg_ref, out_ref, ret_s, gdn_s, hg_s):
    @pl.when(pl.program_id(0) == 0)
    def _():
        ret_s[...] = jnp.zeros_like(ret_s)
        gdn_s[...] = jnp.zeros_like(gdn_s)
        hg_s[...] = jnp.zeros_like(hg_s)

    nsub = proj_ref.shape[1] // CHUNK
    gens = []
    for j in range(nsub):
        rows = pl.ds(CHUNK * j, CHUNK)
        for b in range(proj_ref.shape[0]):
            pr = proj_ref.at[b, rows]
            ou = out_ref.at[b, rows]
            gens.append((2 * j, _gdn_chunk(pr, st_ref.at[b, j], ou, gdn_s.at[b], gdng_ref)))
            gens.append((2 * j, _hgrn_chunk(pr, ou, hg_s.at[b], bd_ref, bdb_ref, hgg_ref)))
            gens.append((2 * j, _ret_chunk(pr, ou, ret_s.at[b], dmat_ref, xi_ref, zeta_ref, rdec_ref,
                                           bd_ref, bdb_ref, retg_ref)))
    rnd = 0
    while gens:
        alive = []
        for d, g in gens:
            if rnd >= d:
                try:
                    next(g)
                except StopIteration:
                    continue
            alive.append((d, g))
        gens = alive
        rnd += 1


def _ret_chunk(proj_ref, out_ref, ret_s, dmat_ref, xi_ref, zeta_ref, rdec_ref, bd_ref, bdb_ref, retg_ref):
    c = CHUNK
    head4 = lax.broadcasted_iota(jnp.int32, (c, RET_W), 1) // RET_HD
    qr = proj_ref[:, OFF_RET:OFF_RET + RET_W]
    kr = proj_ref[:, OFF_RET + RET_W:OFF_RET + 2 * RET_W]
    v = proj_ref[:, OFF_RET + 2 * RET_W:OFF_RET + 3 * RET_W]
    s_ret = ret_s[...]
    o = _mm(qr * xi_ref[...], s_ret)
    scs = [_mm_nt(jnp.where(head4 == h, qr, 0.0), kr) * dmat_ref[h] for h in range(RET_HEADS)]
    yield
    ohs = [_mm(sc, v) for sc in scs]
    ret_s[...] = s_ret * rdec_ref[...] + bd_ref[...] * _mm_tn(kr * zeta_ref[...], v)
    yield
    for h in range(RET_HEADS):
        o = o + jnp.where(head4 == h, ohs[h], 0.0)
    ms = _group_mean_sq(o, bdb_ref[...], RET_HD)
    yield
    out_ref[:, 0:RET_W] = (o * lax.rsqrt(ms + EPS) * retg_ref[...]
                           * proj_ref[:, OFF_RET + 3 * RET_W:OFF_RET + 4 * RET_W])


def _gdn_chunk(proj_ref, st_ref, out_ref, gdn_s, gdng_ref):
    c = CHUNK
    nh = GDN_HEADS
    hs = range(nh)
    r64 = lax.broadcasted_iota(jnp.int32, (c, c), 0)
    c64 = lax.broadcasted_iota(jnp.int32, (c, c), 1)
    causal = r64 >= c64
    strict = r64 > c64
    eye = (r64 == c64).astype(F32)
    qn = [proj_ref[:, OFF_GDN + GDN_HD * h:OFF_GDN + GDN_HD * (h + 1)] for h in hs]
    kn = [proj_ref[:, OFF_GDN + GDN_W + GDN_HD * h:OFF_GDN + GDN_W + GDN_HD * (h + 1)] for h in hs]
    small = proj_ref[:, OFF_SMALL:OFF_SMALL + LANES]
    gcol = [small[:, h:h + 1] for h in hs]
    bcol = [small[:, nh + h:nh + h + 1] for h in hs]
    gamma = [jnp.where(causal, jnp.exp(jnp.minimum(gcol[h] - st_ref[h:h + 1, :], 0.0)), 0.0) for h in hs]
    kk = [_mm_nt(kn[h], kn[h]) for h in hs]
    qk = [_mm_nt(qn[h], kn[h]) * gamma[h] for h in hs]
    yield
    p = [jnp.where(strict, bcol[h] * kk[h] * gamma[h], 0.0) for h in hs]
    t_inv = [eye - p[h] for h in hs]
    eg = [jnp.exp(gcol[h]) for h in hs]
    cv = [proj_ref[:, OFF_GDN + 2 * GDN_W + GDN_HD * h:OFF_GDN + 2 * GDN_W + GDN_HD * (h + 1)] for h in hs]
    rhs = [jnp.concatenate([cv[h] * bcol[h], kn[h] * (bcol[h] * eg[h])], axis=1) for h in hs]
    glast = [small[c - 1:c, h:h + 1] for h in hs]
    kd = [kn[h] * jnp.exp(glast[h] - gcol[h]) for h in hs]
    qd = [qn[h] * eg[h] for h in hs]
    for _ in range(5):
        p = [_mm(x, x) for x in p]
        yield
        t_inv = [t_inv[h] + _mm(t_inv[h], p[h]) for h in hs]
        yield
    sol = [_mm(t_inv[h], rhs[h]) for h in hs]
    yield
    s_g = [gdn_s[h] for h in hs]
    v_new = [sol[h][:, :GDN_HD] - _mm(sol[h][:, GDN_HD:], s_g[h]) for h in hs]
    oh = [_mm(qd[h], s_g[h]) for h in hs]
    yield
    oh = [oh[h] + _mm(qk[h], v_new[h]) for h in hs]
    for h in hs:
        gdn_s[h] = s_g[h] * jnp.exp(glast[h]) + _mm_tn(kd[h], v_new[h])
    yield
    for h in hs:
        msd = jnp.mean(oh[h] * oh[h], axis=-1, keepdims=True)
        dzs = proj_ref[:, OFF_GDN + 3 * GDN_W + GDN_HD * h:OFF_GDN + 3 * GDN_W + GDN_HD * (h + 1)]
        out_ref[:, RET_W + GDN_HD * h:RET_W + GDN_HD * (h + 1)] = oh[h] * lax.rsqrt(msd + EPS) * gdng_ref[...] * dzs


def _hgrn_chunk(proj_ref, out_ref, hg_s, bd_ref, bdb_ref, hgg_ref):
    c = CHUNK
    head4 = lax.broadcasted_iota(jnp.int32, (c, HG_W), 1) // HG_HD
    qh = proj_ref[:, OFF_HG:OFF_HG + HG_W]
    kh = proj_ref[:, OFF_HG + HG_W:OFF_HG + 2 * HG_W]
    hv = proj_ref[:, OFF_HG + 2 * HG_W:OFF_HG + 3 * HG_W]
    gl = proj_ref[:, OFF_HG + 4 * HG_W:OFF_HG + 5 * HG_W]
    row = lax.broadcasted_iota(jnp.int32, (c, HG_W), 0)
    blk = row // SUB_BLOCK
    rin = row % SUB_BLOCK
    nblk = c // SUB_BLOCK
    s_h = hg_s[...]
    o = _mm_nt(qh * jnp.exp(gl), s_h)
    glast = gl[c - 1:c, :]
    hg_s[...] = s_h * jnp.exp(glast) + bd_ref[...] * _mm_tn(hv, kh * jnp.exp(glast - gl))
    yield
    gref = gl[SUB_BLOCK - 1:SUB_BLOCK, :]
    for i in range(2, nblk):
        gref = jnp.where(blk >= i, gl[SUB_BLOCK * i - 1:SUB_BLOCK * i, :], gref)
    qs = qh * jnp.exp(jnp.minimum(gl - gref, 0.0))
    ks = []
    for i in range(1, nblk):
        gi = gl[SUB_BLOCK * i - 1:SUB_BLOCK * i, :]
        ks.append(jnp.where(row < SUB_BLOCK * i, kh * jnp.exp(jnp.minimum(gi - gl, 0.0)), 0.0))
    kcat = jnp.concatenate(ks, axis=1)
    yield
    atts = []
    for h in range(HG_HEADS):
        mh = head4 == h
        qcat = jnp.concatenate([jnp.where(mh & (blk == i), qs, 0.0) for i in range(1, nblk)], axis=1)
        atts.append(_mm_nt(qcat, kcat))
    yield
    offs = [_mm(a, hv) for a in atts]
    yield
    for h in range(HG_HEADS):
        o = o + jnp.where(head4 == h, offs[h], 0.0)
    ps = [qh * kh]
    for j in range(1, SUB_BLOCK):
        e = jnp.exp(jnp.minimum(gl - pltpu.roll(gl, j, 0), 0.0))
        ps.append(jnp.where(rin >= j, qh * e * pltpu.roll(kh, j, 0), 0.0))
        if j % 4 == 0:
            yield
    acat = jnp.dot(jnp.concatenate(ps, axis=0).astype(BF16), bdb_ref[...], preferred_element_type=F32)
    yield
    o = o + acat[0:c] * hv
    for j in range(1, SUB_BLOCK):
        o = o + acat[c * j:c * (j + 1)] * pltpu.roll(hv, j, 0)
        if j % 4 == 0:
            yield
    ms = _group_mean_sq(o, bdb_ref[...], HG_HD)
    yield
    out_ref[:, RET_W + GDN_W:MIX_W] = (o * lax.rsqrt(ms + EPS) * hgg_ref[...]
                                      * proj_ref[:, OFF_HG + 3 * HG_W:OFF_HG + 4 * HG_W])


def _mixer_consts(seq):
    with jax.ensure_compile_time_eval():
        return _mixer_consts_eval(seq)


def _mixer_consts_eval(seq):
    hd = RET_HD
    inv = ROPE_BASE ** (-jnp.arange(0, hd, 2, dtype=F32) / hd)
    ang = jnp.arange(seq, dtype=F32)[:, None] * inv[None, :]
    cos = jnp.cos(ang)
    sin = jnp.sin(ang)
    cos_t = jnp.tile(jnp.concatenate([cos, cos], axis=1), (1, LANES // RET_HD))
    sin_t = jnp.tile(jnp.concatenate([-sin, sin], axis=1), (1, LANES // RET_HD))
    log_gamma = jnp.log(1.0 - 2.0 ** (-5.0 - jnp.arange(RET_HEADS, dtype=F32)))
    idx = jnp.arange(CHUNK, dtype=F32)
    diff = idx[:, None] - idx[None, :]
    cm = diff >= 0
    dmat = jnp.where(cm, jnp.exp(log_gamma[:, None, None] * jnp.where(cm, diff, 0.0)), 0.0)
    zeta = jnp.exp(log_gamma[:, None] * (CHUNK - 1.0 - idx))
    xi = jnp.exp(log_gamma[:, None] * (idx + 1.0))
    zeta_t = jnp.repeat(zeta.T, RET_HD, axis=1)
    xi_t = jnp.repeat(xi.T, RET_HD, axis=1)
    rdec = jnp.repeat(jnp.exp(log_gamma * CHUNK), RET_HD)[None, :]
    hid = np.arange(RET_W) // RET_HD
    bd = jnp.asarray((hid[:, None] == hid[None, :]).astype(np.float32))
    return cos_t, sin_t, dmat, xi_t, zeta_t, rdec, bd, bd.astype(BF16)


def _mixer(proj, st, consts, ret_g, gdn_g, hg_g, batch, seq):
    _, _, dmat, xi_t, zeta_t, rdec, bd, bd_bf16 = consts
    n = seq // CHUNK
    const = lambda shape: pl.BlockSpec(shape, lambda i: (0,) * len(shape))
    in_specs = [
        pl.BlockSpec((batch, MIX_TILE, PROJ_W), lambda i: (0, i, 0)),
        pl.BlockSpec((batch, MIX_TILE // CHUNK, SUBLANES, CHUNK), lambda i: (0, i, 0, 0)),
        const((RET_HEADS, CHUNK, CHUNK)),
        const((CHUNK, RET_W)),
        const((CHUNK, RET_W)),
        const((1, RET_W)),
        const((RET_W, RET_W)),
        const((RET_W, RET_W)),
        const((1, RET_W)),
        const((1, GDN_HD)),
        const((1, HG_W)),
    ]
    return pl.pallas_call(
        _mixer_kernel,
        grid=(seq // MIX_TILE,),
        in_specs=in_specs,
        out_specs=pl.BlockSpec((batch, MIX_TILE, MIX_W), lambda i: (0, i, 0)),
        out_shape=jax.ShapeDtypeStruct((batch, seq, MIX_W), F32),
        scratch_shapes=[
            pltpu.VMEM((batch, RET_W, RET_W), F32),
            pltpu.VMEM((batch, GDN_HEADS, GDN_HD, GDN_HD), F32),
            pltpu.VMEM((batch, HG_W, HG_W), F32),
        ],
        compiler_params=pltpu.CompilerParams(dimension_semantics=("arbitrary",),
                                             vmem_limit_bytes=VMEM_LIMIT),
        name="mixer",
    )(proj.reshape(batch, seq, PROJ_W), st.reshape(batch, n, SUBLANES, CHUNK), dmat, xi_t, zeta_t, rdec, bd,
      bd_bf16, ret_g.reshape(1, RET_W).astype(F32), gdn_g.reshape(1, GDN_HD).astype(F32),
      hg_g.reshape(1, HG_W).astype(F32))


def _post_kernel(mixed_ref, h_ref, wout_ref, g_ref, wr_ref, h1_ref, xn_ref, rout_ref):
    h1 = h_ref[...] + jnp.dot(mixed_ref[...].astype(BF16), wout_ref[...], preferred_element_type=F32)
    h1_ref[...] = h1
    ms = jnp.mean(h1 * h1, axis=-1, keepdims=True)
    xn = h1 * lax.rsqrt(ms + EPS) * g_ref[...]
    _store_rows(xn_ref, xn)
    logits = jnp.dot(xn.astype(BF16), wr_ref[...], preferred_element_type=F32)
    lt = logits.T
    tm = logits.shape[0]
    rowi = lax.broadcasted_iota(jnp.int32, (SUBLANES, tm), 0)
    big = jnp.int32(1 << 20)
    ninf = jnp.float32(-jnp.inf)
    gmask = rowi < N_GROUPS
    gl = jnp.where(gmask, lt[0:SUBLANES, :], ninf)
    eg = jnp.exp(gl - jnp.max(gl, axis=0, keepdims=True))
    pg = eg / jnp.sum(eg, axis=0, keepdims=True)
    pg_top = jnp.max(pg, axis=0, keepdims=True)
    grp = jnp.min(jnp.where(gmask & (pg == pg_top), rowi, big), axis=0, keepdims=True)
    el = lt[SUBLANES:2 * SUBLANES, :]
    for gi in range(1, N_GROUPS):
        el = jnp.where(grp == gi, lt[SUBLANES * (gi + 1):SUBLANES * (gi + 2), :], el)
    ee = jnp.exp(el - jnp.max(el, axis=0, keepdims=True))
    pe = ee / jnp.sum(ee, axis=0, keepdims=True)
    p1 = jnp.max(pe, axis=0, keepdims=True)
    i1 = jnp.min(jnp.where(pe == p1, rowi, big), axis=0, keepdims=True)
    pe2 = jnp.where(rowi == i1, -1.0, pe)
    p2 = jnp.max(pe2, axis=0, keepdims=True)
    i2 = jnp.min(jnp.where(pe2 == p2, rowi, big), axis=0, keepdims=True)
    den = p1 + p2
    w1 = pg_top * (p1 / den)
    w2 = pg_top * (p2 / den)
    e1 = (grp * EXP_PER_GROUP + i1).astype(F32)
    e2 = (grp * EXP_PER_GROUP + i2).astype(F32)
    res = jnp.where(rowi == 0, e1, jnp.where(rowi == 1, e2, jnp.where(rowi == 2, w1, jnp.where(rowi == 3, w2, 0.0))))
    rout_ref[...] = jnp.concatenate([res, jnp.zeros((LANES - SUBLANES, tm), F32)], axis=0).T


def _post(mixed, h, wout_bf16, gain, wr_bf16, tm):
    t = h.shape[0]
    row = pl.BlockSpec((tm, D_MODEL), lambda i: (i, 0))
    return pl.pallas_call(
        _post_kernel,
        grid=(t // tm,),
        in_specs=[row, row,
                  pl.BlockSpec((MIX_W, D_MODEL), lambda i: (0, 0)),
                  pl.BlockSpec((1, D_MODEL), lambda i: (0, 0)),
                  pl.BlockSpec((D_MODEL, LANES), lambda i: (0, 0))],
        out_specs=[row, pl.BlockSpec((tm * ROW_TILES, LANES), lambda i: (i, 0)),
                   pl.BlockSpec((tm, LANES), lambda i: (i, 0))],
        out_shape=[jax.ShapeDtypeStruct((t, D_MODEL), F32), jax.ShapeDtypeStruct((t * ROW_TILES, LANES), F32),
                   jax.ShapeDtypeStruct((t, LANES), F32)],
        compiler_params=pltpu.CompilerParams(dimension_semantics=("arbitrary",),
                                             vmem_limit_bytes=VMEM_LIMIT),
        name="post",
    )(mixed, h, wout_bf16, gain.reshape(1, D_MODEL), wr_bf16)


def _moe_kernel(bexp_ref, nused_ref, tok_ref, tokn_ref, dstp_ref, dst_ref, xn_hbm,
                wg_ref, wu_ref, wd_ref, y_hbm, xbuf, ybuf, wgb, wub, wdb, gsem, ssem, *, dummy_row):
    i = pl.program_id(0)
    nb = pl.num_programs(0)
    nused = nused_ref[0]
    blk = MOE_BLOCK
    row_c = lax.rem(i, SUBLANES)
    row_n = lax.rem(jnp.minimum(i + 1, nb - 1), SUBLANES)
    row_p = lax.rem(jnp.maximum(i - 1, 0), SUBLANES)
    n_piece = D_EXPERT // LANES
    per = blk // n_piece

    def gather_copy(t, r, s):
        return pltpu.make_async_copy(xn_hbm.at[pl.ds(pl.multiple_of(t, ROW_TILES), ROW_TILES)],
                                     xbuf.at[s, pl.ds(ROW_TILES * r, ROW_TILES)], gsem.at[s])

    def scatter_copy(d, r, s):
        return pltpu.make_async_copy(ybuf.at[s, pl.ds(ROW_TILES * r, ROW_TILES)],
                                     y_hbm.at[pl.ds(pl.multiple_of(d, ROW_TILES), ROW_TILES)], ssem.at[s])

    @pl.when(i == 0)
    def _():
        ybuf[1] = jnp.zeros((blk * ROW_TILES, LANES), F32)
        for r in range(blk):
            gather_copy(tok_ref[row_c, r], r, 0).start(priority=r % 2)

    @pl.when((i < nused) & ((i == 0) | (bexp_ref[i] != bexp_ref[jnp.maximum(i - 1, 0)])))
    def _():
        wgb[...] = wg_ref[...].astype(BF16)
        wub[...] = wu_ref[...].astype(BF16)
        wdb[...] = wd_ref[...].astype(BF16)

    def step(s):
        for r in range(blk):
            gather_copy(0, r, s).wait()
        first = i == 0
        x = _load_rows(xbuf.at[s], blk).astype(BF16)
        parts = []
        for c in range(n_piece):
            cs = slice(LANES * c, LANES * (c + 1))
            hg = jnp.dot(x, wgb[:, cs], preferred_element_type=F32)
            hu = jnp.dot(x, wub[:, cs], preferred_element_type=F32)
            parts.append((_silu(hg) * hu).astype(BF16))
            for r in range(per * c, per * (c + 1)):
                gather_copy(tokn_ref[row_n, r], r, 1 - s).start(priority=r % 2)
                scatter_copy(jnp.where(first, dummy_row + ROW_TILES * r, dstp_ref[row_p, r]), r, 1 - s).start(priority=r % 2)
        hb = jnp.concatenate(parts, axis=1)
        _store_rows(ybuf.at[s], jnp.dot(hb, wdb[...], preferred_element_type=F32))
        for r in range(blk):
            scatter_copy(0, r, 1 - s).wait()

        @pl.when(i == nused - 1)
        def _():
            for r in range(blk):
                scatter_copy(dst_ref[row_c, r], r, s).start(priority=r % 2)
            for r in range(blk):
                scatter_copy(0, r, s).wait()
            for r in range(blk):
                gather_copy(0, r, 1 - s).wait()

    @pl.when((i < nused) & (lax.rem(i, 2) == 0))
    def _():
        step(0)

    @pl.when((i < nused) & (lax.rem(i, 2) == 1))
    def _():
        step(1)


def _experts(xn, tok, dst, bexp, nused, w_gate, w_up, w_down, layer):
    t = xn.shape[0] // ROW_TILES
    nb = bexp.shape[0]
    blk = MOE_BLOCK
    smem_blk = lambda f: pl.BlockSpec((SUBLANES, blk), f, memory_space=pltpu.SMEM)
    grid_spec = pltpu.PrefetchScalarGridSpec(
        num_scalar_prefetch=2,
        grid=(nb,),
        in_specs=[
            smem_blk(lambda i, be, nu: (i // SUBLANES, 0)),
            smem_blk(lambda i, be, nu: (jnp.minimum(i + 1, nb - 1) // SUBLANES, 0)),
            smem_blk(lambda i, be, nu: (jnp.maximum(i - 1, 0) // SUBLANES, 0)),
            smem_blk(lambda i, be, nu: (i // SUBLANES, 0)),
            pl.BlockSpec(memory_space=pl.ANY),
            pl.BlockSpec((None, None, D_MODEL, D_EXPERT), lambda i, be, nu: (layer, be[i], 0, 0)),
            pl.BlockSpec((None, None, D_MODEL, D_EXPERT), lambda i, be, nu: (layer, be[i], 0, 0)),
            pl.BlockSpec((None, None, D_EXPERT, D_MODEL), lambda i, be, nu: (layer, be[i], 0, 0)),
        ],
        out_specs=pl.BlockSpec(memory_space=pl.ANY),
        scratch_shapes=[
            pltpu.VMEM((2, blk * ROW_TILES, LANES), F32),
            pltpu.VMEM((2, blk * ROW_TILES, LANES), F32),
            pltpu.VMEM((D_MODEL, D_EXPERT), BF16),
            pltpu.VMEM((D_MODEL, D_EXPERT), BF16),
            pltpu.VMEM((D_EXPERT, D_MODEL), BF16),
            pltpu.SemaphoreType.DMA((2,)),
            pltpu.SemaphoreType.DMA((2,)),
        ],
    )
    tok2 = tok.reshape(nb, blk)
    dst2 = dst.reshape(nb, blk)
    return pl.pallas_call(
        functools.partial(_moe_kernel, dummy_row=TOP_K * t * ROW_TILES),
        grid_spec=grid_spec,
        out_shape=jax.ShapeDtypeStruct(((TOP_K * t + blk) * ROW_TILES, LANES), F32),
        compiler_params=pltpu.CompilerParams(dimension_semantics=("arbitrary",),
                                             vmem_limit_bytes=VMEM_LIMIT),
        name="experts",
    )(bexp, nused, tok2, tok2, dst2, dst2, xn, w_gate, w_up, w_down)


def _dispatch(rout, t):
    n_assign = t * TOP_K
    e_flat = rout[:, 0:TOP_K].astype(jnp.int32).reshape(n_assign)
    e_sorted, order = lax.sort_key_val(e_flat, jnp.arange(n_assign, dtype=jnp.int32))
    start_raw = jnp.searchsorted(e_sorted, jnp.arange(N_EXPERTS + 1, dtype=jnp.int32), side='left',
                                 method='compare_all').astype(jnp.int32)
    counts = start_raw[1:] - start_raw[:-1]
    padded = (counts + (MOE_BLOCK - 1)) & ~(MOE_BLOCK - 1)
    ends_pad = jnp.cumsum(padded)
    start_pad = ends_pad - padded
    n_blocks = (n_assign + N_EXPERTS * (MOE_BLOCK - 1) + MOE_BLOCK - 1) // MOE_BLOCK
    n_blocks = (n_blocks + SUBLANES - 1) // SUBLANES * SUBLANES
    blk_start = jnp.arange(n_blocks, dtype=jnp.int32) * MOE_BLOCK
    bexp = jnp.minimum(jnp.searchsorted(ends_pad, blk_start, side='right', method='compare_all'),
                       N_EXPERTS - 1).astype(jnp.int32)
    slot = jnp.arange(n_blocks * MOE_BLOCK, dtype=jnp.int32)
    se = jnp.repeat(bexp, MOE_BLOCK)
    rank = slot - start_pad[se]
    valid = (rank >= 0) & (rank < counts[se])
    a = order[jnp.clip(start_raw[se] + rank, 0, n_assign - 1)]
    a_tok = lax.shift_right_logical(a, 1)
    tok = (jnp.where(valid, a_tok, 0) * ROW_TILES).astype(jnp.int32)
    dst = (jnp.where(valid, (a & 1) * t + a_tok, TOP_K * t + (slot & (MOE_BLOCK - 1))) * ROW_TILES).astype(jnp.int32)
    nused = lax.shift_right_logical(ends_pad[-1:], MOE_BLOCK_LOG2).astype(jnp.int32)
    return tok, dst, bexp, nused


def _final_kernel(h_ref, ya_ref, yb_ref, rt_ref, g_ref, o_ref):
    x = h_ref[...] + _combine(ya_ref, yb_ref, rt_ref)
    ms = jnp.mean(x * x, axis=-1, keepdims=True)
    o_ref[...] = x * lax.rsqrt(ms + EPS) * g_ref[...]


def _final(h1, y2, rout, gain, tm):
    t = h1.shape[0]
    row = pl.BlockSpec((tm, D_MODEL), lambda i: (i, 0))
    return pl.pallas_call(
        _final_kernel,
        grid=(t // tm,),
        in_specs=[row,
                  pl.BlockSpec((tm * ROW_TILES, LANES), lambda i: (i, 0)),
                  pl.BlockSpec((tm * ROW_TILES, LANES), lambda i: (t // tm + i, 0)),
                  pl.BlockSpec((tm, LANES), lambda i: (i, 0)),
                  pl.BlockSpec((1, D_MODEL), lambda i: (0, 0))],
        out_specs=row,
        out_shape=jax.ShapeDtypeStruct((t, D_MODEL), F32),
        compiler_params=pltpu.CompilerParams(dimension_semantics=("arbitrary",)),
        name="final_norm",
    )(h1, y2, y2, rout, gain.reshape(1, D_MODEL))


def _prep_w_kernel(w_ref, o_ref):
    a = 4 * RET_W + 3 * GDN_W
    n_small = 2 * GDN_HEADS
    rest = w_ref.shape[1] - a - n_small
    o_ref[:, 0:a] = w_ref[:, 0:a].astype(BF16)
    o_ref[:, a:a + rest] = w_ref[:, a + n_small:a + n_small + rest].astype(BF16)
    lane = lax.broadcasted_iota(jnp.int32, (w_ref.shape[0], LANES), 1)
    o_ref[:, W_SMALL:W_SMALL + LANES] = jnp.where(lane < n_small, w_ref[:, a:a + LANES], 0.0).astype(BF16)


def _prep_w_in(w, layer):
    rows = 128
    return pl.pallas_call(
        _prep_w_kernel,
        grid=(w.shape[1] // rows,),
        in_specs=[pl.BlockSpec((None, rows, w.shape[2]), lambda i: (layer, i, 0))],
        out_specs=pl.BlockSpec((rows, W_COLS), lambda i: (i, 0)),
        out_shape=jax.ShapeDtypeStruct((w.shape[1], W_COLS), BF16),
        name="prep_w_in",
    )(w)


def kernel(x, w_in, w_out, gdn_conv, gdn_a_log, gdn_dt_bias, ret_norm, gdn_norm, hgrn_norm,
           hgrn_lower_bounds, attn_norm, ffn_norm, router_group, router_expert,
           w_gate, w_up, w_down, final_norm):
    batch, seq, dm = x.shape
    t = batch * seq
    tm = min(512, seq)
    lb_all = jnp.cumsum(jax.nn.softmax(hgrn_lower_bounds.astype(F32), axis=0), axis=0)
    lb_all = lb_all - lb_all[0:1]
    consts = _mixer_consts(seq)
    h = x.reshape(t, dm)
    h1 = None
    y2 = None
    rout = None
    for l in range(DEPTH):
        res = _inproj(h if l == 0 else h1, y2, rout, attn_norm[l], _prep_w_in(w_in, l), consts, gdn_conv[l],
                      gdn_a_log[l], gdn_dt_bias[l], lb_all[l], seq, tm)
        if l == 0:
            proj, st = res
        else:
            proj, st, h = res
        mixed = _mixer(proj, st, consts, ret_norm[l], gdn_norm[l], hgrn_norm[l], batch, seq).reshape(t, MIX_W)
        wr = jnp.concatenate([router_group[l], jnp.zeros((dm, SUBLANES - N_GROUPS), F32), router_expert[l],
                              jnp.zeros((dm, LANES - SUBLANES - N_EXPERTS), F32)], axis=1).astype(BF16)
        h1, xn, rout = _post(mixed, h, w_out[l].astype(BF16), ffn_norm[l], wr, tm)
        tok, dst, bexp, nused = _dispatch(rout, t)
        y2 = _experts(xn, tok, dst, bexp, nused, w_gate, w_up, w_down, l)
    out = _final(h1, y2, rout, final_norm, tm)
    return out.reshape(batch, seq, dm)
```
